```python
import jax, jax.numpy as jnp
from jax import lax
import numpy as np

D_MODEL = 2048
BATCH = 4
SEQ = 2048
DEPTH = 4
DEC_BATCH = 8
DEC_SEQ = 4
PAST_LEN = 16384
PAGE_SIZE = 128

N_HEADS = 16
N_KV_HEADS = 4
HEAD_DIM = D_MODEL // N_HEADS
KV_GROUP = N_HEADS // N_KV_HEADS
ROT_DIM = HEAD_DIM // 4
ROPE_THETA = 500000.0
MOBA_BLOCK = 256
MOBA_TOPK = 3
MOBA_Q_CHUNK = 16
SB_Q_BLOCK = 128
POOL_WINDOWS = (2, 4, 8, 16)
POOL_GROUP = D_MODEL // len(POOL_WINDOWS)
POOL_STATE = max(POOL_WINDOWS) - 1
D_FF = 7 * D_MODEL // 2
N_EXPERTS = 8
TOP_K = 2
N_MIXERS = 3
ALPHA = (2 * DEPTH) ** 0.25
BETA = (8 * DEPTH) ** -0.25
LN_EPS = 1e-5

kernel_name = "hybrid_moba_pool_stickbreak_decode_step"


def _layer_norm(x, g, b):
    xf = x.astype(jnp.float32)
    mu = jnp.mean(xf, axis=-1, keepdims=True)
    var = jnp.mean(jnp.square(xf - mu), axis=-1, keepdims=True)
    y = (xf - mu) * lax.rsqrt(var + LN_EPS) * g.astype(jnp.float32) + b.astype(jnp.float32)
    return y.astype(x.dtype)


def _rope(x, pos):
    inv = ROPE_THETA ** (-jnp.arange(0, ROT_DIM, 2, dtype=jnp.float32) / ROT_DIM)
    ang = pos.astype(jnp.float32)[:, None] * inv[None, :]
    cos = jnp.cos(ang)[None, :, None, :]
    sin = jnp.sin(ang)[None, :, None, :]
    half = ROT_DIM // 2
    xr = x[..., :ROT_DIM].astype(jnp.float32)
    x1, x2 = xr[..., :half], xr[..., half:]
    rot = jnp.concatenate([x1 * cos - x2 * sin, x2 * cos + x1 * sin], axis=-1).astype(x.dtype)
    return jnp.concatenate([rot, x[..., ROT_DIM:]], axis=-1)


def _gather_pages(pool, page_table):
    pages = pool[page_table]
    b, n, p = pages.shape[:3]
    return pages.reshape((b, n * p) + pages.shape[3:])


def _map_query_blocks(fn, block, q_pos, *arrs):
    t_q = q_pos.shape[0]
    blk = min(block, t_q)
    n_blk = -(-t_q // blk)
    pad = n_blk * blk - t_q

    def split(a):
        a = jnp.pad(a, [(0, 0), (0, pad)] + [(0, 0)] * (a.ndim - 2), mode="edge")
        a = a.reshape((a.shape[0], n_blk, blk) + a.shape[2:])
        return jnp.moveaxis(a, 1, 0)

    pos_b = jnp.pad(q_pos, (0, pad), mode="edge").reshape(n_blk, blk)
    out = lax.map(lambda xs: fn(xs[0], *xs[1:]), (pos_b,) + tuple(split(a) for a in arrs))
    out = jnp.moveaxis(out, 0, 1)
    out = out.reshape((out.shape[0], n_blk * blk) + out.shape[3:])
    return out[:, :t_q]


def _moba_attention(q, k, v, q_pos):
    B, t_q, H, hd = q.shape
    L = k.shape[1]
    n_blk = -(-L // MOBA_BLOCK)
    pad = n_blk * MOBA_BLOCK - L

    def blocks(a):
        a = jnp.pad(a, ((0, 0), (0, pad), (0, 0), (0, 0)))
        return a.reshape(B, n_blk, MOBA_BLOCK, N_KV_HEADS, hd).transpose(0, 3, 1, 2, 4)

    kb, vb = blocks(k), blocks(v)
    k_mean = jnp.repeat(jnp.mean(kb.astype(jnp.float32), axis=3), KV_GROUP, axis=1)
    own = q_pos // MOBA_BLOCK
    gate = jnp.einsum("bthd,bhnd->bthn", q.astype(jnp.float32), k_mean)
    fully_past = jnp.arange(n_blk)[None, :] < own[:, None]
    gate = jnp.where(fully_past[None, :, None, :], gate, -jnp.inf)
    own_idx = jnp.broadcast_to(own[None, :, None, None], (B, t_q, H, 1)).astype(jnp.int32)
    own_ok = jnp.ones((B, t_q, H, 1), dtype=bool)
    n_sel = min(MOBA_TOPK, n_blk - 1)
    if n_sel > 0:
        _, top_idx = lax.top_k(gate, n_sel)
        top_idx = top_idx.astype(jnp.int32)
        idx = jnp.concatenate([top_idx, own_idx], axis=-1)
        valid = jnp.concatenate([top_idx < own[None, :, None, None], own_ok], axis=-1)
    else:
        idx, valid = own_idx, own_ok
    n_sl = idx.shape[-1]
    b_ix = jnp.arange(B)[:, None, None, None]
    g_ix = (jnp.arange(H) // KV_GROUP)[None, None, :, None]
    offs = jnp.arange(MOBA_BLOCK)
    scale = HEAD_DIM ** -0.5

    def attend(pc, qc, ic, vc):
        qn = qc.shape[1]
        kg = kb[b_ix, g_ix, ic].reshape(B, qn, H, n_sl * MOBA_BLOCK, hd)
        vg = vb[b_ix, g_ix, ic].reshape(B, qn, H, n_sl * MOBA_BLOCK, hd)
        s = jnp.einsum("bqhd,bqhkd->bqhk", qc, kg).astype(jnp.float32) * scale
        kpos = (ic[..., None] * MOBA_BLOCK + offs).reshape(B, qn, H, n_sl * MOBA_BLOCK)
        m = jnp.repeat(vc, MOBA_BLOCK, axis=-1) & (kpos <= pc[None, :, None, None])
        p = jax.nn.softmax(jnp.where(m, s, -jnp.inf), axis=-1)
        return jnp.einsum("bqhk,bqhkd->bqhd", p.astype(vg.dtype), vg)

    return _map_query_blocks(attend, MOBA_Q_CHUNK, q_pos, q, idx, valid)


def _stick_breaking_attention(q, k, v, q_pos):
    B, t_q, H, hd = q.shape
    L = k.shape[1]
    k_pos = jnp.arange(L)
    scale = HEAD_DIM ** -0.5

    def block(pc, qc):
        qb = qc.shape[1]
        qg = qc.reshape(B, qb, N_KV_HEADS, KV_GROUP, hd)
        z = jnp.einsum("bqgrd,bkgd->bgrqk", qg, k).astype(jnp.float32) * scale
        causal = k_pos[None, :] < pc[:, None]
        log_beta = jax.nn.log_sigmoid(z)
        log_keep = jnp.where(causal, jax.nn.log_sigmoid(-z), 0.0)
        later = lax.cumsum(log_keep, axis=4, reverse=True) - log_keep
        w = jnp.where(causal, jnp.exp(log_beta + later), 0.0)
        o = jnp.einsum("bgrqk,bkgd->bqgrd", w.astype(v.dtype), v)
        return o.reshape(B, qb, H, hd)

    return _map_query_blocks(block, SB_Q_BLOCK, q_pos, q)


def _pool_mixer(x, prefix, pos, w_pool, scale):
    B, T, D = x.shape
    P = prefix.shape[1]
    xf = jnp.concatenate([prefix, x], axis=1).astype(jnp.float32)
    cs = jnp.concatenate([jnp.zeros((B, 1, D), jnp.float32), lax.cumsum(xf, axis=1)], axis=1)
    end = cs[:, P + 1:]
    means = []
    for g, w in enumerate(POOL_WINDOWS):
        c0, c1 = g * POOL_GROUP, (g + 1) * POOL_GROUP
        start = cs[:, P + 1 - w:P + 1 - w + T, c0:c1]
        cnt = jnp.minimum(w, pos + 1).astype(jnp.float32)[None, :, None]
        means.append((end[..., c0:c1] - start) / cnt)
    mixed = (jnp.concatenate(means, axis=-1) - xf[:, P:]).astype(x.dtype)
    mixed = mixed.reshape(B, T, len(POOL_WINDOWS), POOL_GROUP)
    out = jnp.einsum("btgc,gce->btge", mixed, w_pool).reshape(B, T, D)
    return out * scale


def _swiglu(x, wg, wu, wd):
    return (jax.nn.silu(x @ wg) * (x @ wu)) @ wd


def _moe_swiglu(x, w_router, wg, wu, wd):
    B, T, D = x.shape
    xf = x.reshape(B * T, D)
    logits = (xf @ w_router).astype(jnp.float32)
    top_val, top_idx = lax.top_k(logits, TOP_K)
    gate = jax.nn.softmax(top_val, axis=-1)
    dense_gate = jnp.sum(jax.nn.one_hot(top_idx, N_EXPERTS, dtype=jnp.float32) * gate[..., None], axis=1)
    out = jnp.zeros_like(xf)
    for e in range(N_EXPERTS):
        y = _swiglu(xf, wg[e], wu[e], wd[e])
        out = out + dense_gate[:, e:e + 1].astype(y.dtype) * y
    return out.reshape(B, T, D)


def _trunk(x, pos0, past, page_table, mix_w, ffn_w, norm_w):
    B, T, _ = x.shape
    pos = pos0 + jnp.arange(T, dtype=jnp.int32)
    new_state = []
    for i in range(DEPTH):
        kind = i % N_MIXERS
        if kind == 1:
            prefix = past[i]
            h = _pool_mixer(x, prefix, pos, *mix_w[i])
            new_state.append((jnp.concatenate([prefix, x], axis=1)[:, -POOL_STATE:],))
        else:
            wq, wk, wv, wo = mix_w[i]
            q = (x @ wq).reshape(B, T, N_HEADS, HEAD_DIM)
            k = (x @ wk).reshape(B, T, N_KV_HEADS, HEAD_DIM)
            v = (x @ wv).reshape(B, T, N_KV_HEADS, HEAD_DIM)
            if kind == 0:
                q, k = _rope(q, pos), _rope(k, pos)
            pk, pv = past[i]
            if pk is None:
                k_all, v_all = k, v
            else:
                k_all = jnp.concatenate([_gather_pages(pk, page_table), k], axis=1)
                v_all = jnp.concatenate([_gather_pages(pv, page_table), v], axis=1)
            if kind == 0:
                o = _moba_attention(q, k_all, v_all, pos)
            else:
                o = _stick_breaking_attention(q, k_all, v_all, pos)
            h = o.reshape(B, T, N_HEADS * HEAD_DIM) @ wo
            new_state.append((k, v))
        g1, b1, g2, b2 = norm_w[i]
        x = _layer_norm(ALPHA * x + h, g1, b1)
        f = _swiglu(x, *ffn_w[i]) if i % 2 == 0 else _moe_swiglu(x, *ffn_w[i])
        x = _layer_norm(ALPHA * x + f, g2, b2)
    return x, new_state


def setup_inputs(seed: int = 0) -> dict:
    key = jax.random.key(seed)
    keys = jax.random.split(key, 96)
    counter = [0]
    f32 = jnp.float32

    def nrm(shape, scale):
        kk = keys[counter[0]]
        counter[0] += 1
        return jax.random.normal(kk, shape, f32) * scale

    n_pages = PAST_LEN // PAGE_SIZE
    n_used = DEC_BATCH * n_pages
    n_phys = n_used + max(1, n_used // 4)
    kv_shape = (n_phys, PAGE_SIZE, N_KV_HEADS, HEAD_DIM)
    attn_w = N_HEADS * HEAD_DIM
    kv_w = N_KV_HEADS * HEAD_DIM
    inp = {}
    inp["x_prompt"] = nrm((BATCH, SEQ, D_MODEL), 1.0)
    inp["x_sample"] = nrm((DEC_BATCH, DEC_SEQ, D_MODEL), 1.0)
    inp["cache_k_l0"] = nrm(kv_shape, 1.0)
    inp["cache_v_l0"] = nrm(kv_shape, 1.0)
    inp["state_pool_l1"] = nrm((DEC_BATCH, POOL_STATE, D_MODEL), 1.0)
    inp["cache_k_l2"] = nrm(kv_shape, 1.0)
    inp["cache_v_l2"] = nrm(kv_shape, 1.0)
    inp["cache_k_l3"] = nrm(kv_shape, 1.0)
    inp["cache_v_l3"] = nrm(kv_shape, 1.0)
    perm = jax.random.permutation(keys[counter[0]], n_phys)[:n_used]
    counter[0] += 1
    inp["page_table"] = perm.reshape(DEC_BATCH, n_pages).astype(jnp.int32)

    def attn(p):
        inp[p + "wq"] = nrm((D_MODEL, attn_w), D_MODEL ** -0.5)
        inp[p + "wk"] = nrm((D_MODEL, kv_w), D_MODEL ** -0.5)
        inp[p + "wv"] = nrm((D_MODEL, kv_w), D_MODEL ** -0.5)
        inp[p + "wo"] = nrm((attn_w, D_MODEL), attn_w ** -0.5 * BETA)

    def norm(p):
        inp[p + "_g"] = 1.0 + nrm((D_MODEL,), 0.02)
        inp[p + "_b"] = nrm((D_MODEL,), 0.02)

    def dense(p):
        inp[p + "ffn_wg"] = nrm((D_MODEL, D_FF), D_MODEL ** -0.5)
        inp[p + "ffn_wu"] = nrm((D_MODEL, D_FF), D_MODEL ** -0.5)
        inp[p + "ffn_wd"] = nrm((D_FF, D_MODEL), D_FF ** -0.5 * BETA)

    def moe(p):
        inp[p + "router"] = nrm((D_MODEL, N_EXPERTS), D_MODEL ** -0.5)
        inp[p + "moe_wg"] = nrm((N_EXPERTS, D_MODEL, D_FF), D_MODEL ** -0.5)
        inp[p + "moe_wu"] = nrm((N_EXPERTS, D_MODEL, D_FF), D_MODEL ** -0.5)
        inp[p + "moe_wd"] = nrm((N_EXPERTS, D_FF, D_MODEL), D_FF ** -0.5 * BETA)

    attn("l0_"); norm("l0_ln1"); dense("l0_"); norm("l0_ln2")
    inp["l1_pool_w"] = nrm((len(POOL_WINDOWS), POOL_GROUP, POOL_GROUP), POOL_GROUP ** -0.5 * BETA)
    inp["l1_pool_scale"] = 1.0 + nrm((D_MODEL,), 0.1)
    norm("l1_ln1"); moe("l1_"); norm("l1_ln2")
    attn("l2_"); norm("l2_ln1"); dense("l2_"); norm("l2_ln2")
    attn("l3_"); norm("l3_ln1"); moe("l3_"); norm("l3_ln2")
    return inp


def reference(x_prompt, x_sample, cache_k_l0, cache_v_l0, state_pool_l1, cache_k_l2, cache_v_l2,
              cache_k_l3, cache_v_l3, page_table,
              l0_wq, l0_wk, l0_wv, l0_wo, l0_ln1_g, l0_ln1_b, l0_ffn_wg, l0_ffn_wu, l0_ffn_wd, l0_ln2_g, l0_ln2_b,
              l1_pool_w, l1_pool_scale, l1_ln1_g, l1_ln1_b, l1_router, l1_moe_wg, l1_moe_wu, l1_moe_wd, l1_ln2_g, l1_ln2_b,
              l2_wq, l2_wk, l2_wv, l2_wo, l2_ln1_g, l2_ln1_b, l2_ffn_wg, l2_ffn_wu, l2_ffn_wd, l2_ln2_g, l2_ln2_b,
              l3_wq, l3_wk, l3_wv, l3_wo, l3_ln1_g, l3_ln1_b, l3_router, l3_moe_wg, l3_moe_wu, l3_moe_wd, l3_ln2_g, l3_ln2_b):
    mix_w = [(l0_wq, l0_wk, l0_wv, l0_wo), (l1_pool_w, l1_pool_scale),
             (l2_wq, l2_wk, l2_wv, l2_wo), (l3_wq, l3_wk, l3_wv, l3_wo)]
    ffn_w = [(l0_ffn_wg, l0_ffn_wu, l0_ffn_wd), (l1_router, l1_moe_wg, l1_moe_wu, l1_moe_wd),
             (l2_ffn_wg, l2_ffn_wu, l2_ffn_wd), (l3_router, l3_moe_wg, l3_moe_wu, l3_moe_wd)]
    norm_w = [(l0_ln1_g, l0_ln1_b, l0_ln2_g, l0_ln2_b), (l1_ln1_g, l1_ln1_b, l1_ln2_g, l1_ln2_b),
              (l2_ln1_g, l2_ln1_b, l2_ln2_g, l2_ln2_b), (l3_ln1_g, l3_ln1_b, l3_ln2_g, l3_ln2_b)]
    prompt_past = [(None, None), jnp.zeros((x_prompt.shape[0], POOL_STATE, D_MODEL), x_prompt.dtype),
                   (None, None), (None, None)]
    sample_past = [(cache_k_l0, cache_v_l0), state_pool_l1, (cache_k_l2, cache_v_l2), (cache_k_l3, cache_v_l3)]
    y_prompt, st_p = _trunk(x_prompt, 0, prompt_past, None, mix_w, ffn_w, norm_w)
    y_sample, st_s = _trunk(x_sample, PAST_LEN, sample_past, page_table, mix_w, ffn_w, norm_w)
    (kp0, vp0), (pp1,), (kp2, vp2), (kp3, vp3) = st_p
    (ks0, vs0), (ps1,), (ks2, vs2), (ks3, vs3) = st_s
    return (y_prompt, y_sample, kp0, vp0, ks0, vs0, pp1, ps1, kp2, vp2, ks2, vs2, kp3, vp3, ks3, vs3)
```

```python
import functools
import math

import jax
import jax.numpy as jnp
from jax import lax
from jax.experimental import pallas as pl
from jax.experimental.pallas import tpu as pltpu

F32 = jnp.float32
BF16 = jnp.bfloat16

D_MODEL = 2048
DEPTH = 4
PAST_LEN = 16384
PAGE_SIZE = 128
N_HEADS = 16
N_KV_HEADS = 4
HEAD_DIM = D_MODEL // N_HEADS
KV_GROUP = N_HEADS // N_KV_HEADS
KV_WIDTH = N_KV_HEADS * HEAD_DIM
ROT_DIM = HEAD_DIM // 4
ROPE_THETA = 500000.0
MOBA_BLOCK = 256
MOBA_TOPK = 3
POOL_WINDOWS = (2, 4, 8, 16)
POOL_GROUP = D_MODEL // len(POOL_WINDOWS)
POOL_STATE = max(POOL_WINDOWS) - 1
N_EXPERTS = 8
TOP_K = 2
ALPHA = (2 * DEPTH) ** 0.25
LN_EPS = 1e-5
ATTN_SCALE = HEAD_DIM ** -0.5

LANES = 128
SUBLANES = 8
VMEM_LIMIT_BYTES = 56 * 1024 * 1024

ROW_TILE = 1056
LN_ROW_TILE = 528
FF_TILE = 512
GROUP_TILE = 512
MOE_DOWN_K_TILE = 1024
COMBINE_TILE = 264
DISPATCH_STEPS = 16
ATT_BLOCK = 256
NEG_BIG = -1e30
EXP_ZERO_BELOW = -104.0


def _cparams(sem):
    return pltpu.CompilerParams(dimension_semantics=sem, vmem_limit_bytes=VMEM_LIMIT_BYTES)


def _dot(a, b):
    return jnp.dot(a, b, preferred_element_type=F32)


def _dot_nt(a, b):
    return lax.dot_general(a, b, (((1,), (1,)), ((), ())), preferred_element_type=F32)


def _split_bf16(x, n):
    parts, r = [], x
    for _ in range(n):
        p = r.astype(BF16)
        parts.append(p)
        r = r - p.astype(F32)
    return parts


def _dot_f32(a, b, nt=False):
    a3, b3 = _split_bf16(a, 3), _split_bf16(b, 3)
    dot = _dot_nt if nt else _dot
    out = None
    for i, j in ((0, 2), (1, 1), (2, 0), (0, 1), (1, 0), (0, 0)):
        t = dot(a3[i], b3[j])
        out = t if out is None else out + t
    return out


def _layer_norm_rows(y, g, b):
    mu = jnp.mean(y, axis=-1, keepdims=True)
    d = y - mu
    var = jnp.mean(d * d, axis=-1, keepdims=True)
    return d * lax.rsqrt(var + LN_EPS) * g + b


def _proj_kernel(x_ref, w_ref, o_ref):
    o_ref[...] = _dot(x_ref[...].astype(BF16), w_ref[...].astype(BF16))


def _proj_rope_kernel(x_ref, w_ref, ca_ref, sb_ref, sc_ref, o_ref):
    y = _dot(x_ref[...].astype(BF16), w_ref[...].astype(BF16))
    half = ROT_DIM // 2
    ca, sb, sc = ca_ref[...], sb_ref[...], sc_ref[...]
    for h in range(y.shape[1] // HEAD_DIM):
        yh = y[:, h * HEAD_DIM:(h + 1) * HEAD_DIM]
        up = pltpu.roll(yh, HEAD_DIM - half, axis=1)
        down = pltpu.roll(yh, half, axis=1)
        o_ref[:, h * HEAD_DIM:(h + 1) * HEAD_DIM] = yh * ca + up * sb + down * sc


def _project(x, w, rope=None, col_tile=512):
    m, kdim = x.shape
    n = w.shape[1]
    tn = min(col_tile, n)
    grid = (m // ROW_TILE, n // tn)
    x_spec = pl.BlockSpec((ROW_TILE, kdim), lambda i, j: (i, 0))
    w_spec = pl.BlockSpec((kdim, tn), lambda i, j: (0, j))
    o_spec = pl.BlockSpec((ROW_TILE, tn), lambda i, j: (i, j))
    if rope is None:
        return pl.pallas_call(
            _proj_kernel, grid=grid, in_specs=[x_spec, w_spec], out_specs=o_spec,
            out_shape=jax.ShapeDtypeStruct((m, n), F32),
            compiler_params=_cparams(("parallel", "arbitrary")), name="project")(x, w)
    t_spec = pl.BlockSpec((ROW_TILE, HEAD_DIM), lambda i, j: (i, 0))
    return pl.pallas_call(
        _proj_rope_kernel, grid=grid, in_specs=[x_spec, w_spec, t_spec, t_spec, t_spec],
        out_specs=o_spec, out_shape=jax.ShapeDtypeStruct((m, n), F32),
        compiler_params=_cparams(("parallel", "arbitrary")), name="project_rope")(x, w, *rope)


def _gateup_kernel(x_ref, wg_ref, wu_ref, h_ref):
    x = x_ref[...].astype(BF16)
    a = _dot(x, wg_ref[...].astype(BF16))
    u = _dot(x, wu_ref[...].astype(BF16))
    h_ref[...] = (a * (1.0 / (1.0 + jnp.exp(-a))) * u).astype(h_ref.dtype)


def _gate_up(x, wg, wu):
    m, kdim = x.shape
    n = wg.shape[1]
    grid = (m // ROW_TILE, n // FF_TILE)
    return pl.pallas_call(
        _gateup_kernel, grid=grid,
        in_specs=[pl.BlockSpec((ROW_TILE, kdim), lambda i, j: (i, 0)),
                  pl.BlockSpec((kdim, FF_TILE), lambda i, j: (0, j)),
                  pl.BlockSpec((kdim, FF_TILE), lambda i, j: (0, j))],
        out_specs=pl.BlockSpec((ROW_TILE, FF_TILE), lambda i, j: (i, j)),
        out_shape=jax.ShapeDtypeStruct((m, n), BF16),
        compiler_params=_cparams(("parallel", "arbitrary")), name="gate_up")(x, wg, wu)


def _down_ln_kernel(h_ref, w_ref, res_ref, g_ref, b_ref, y_ref, yb_ref, acc_ref):
    k = pl.program_id(1)

    @pl.when(k == 0)
    def _():
        acc_ref[...] = jnp.zeros_like(acc_ref)

    acc_ref[...] += _dot(h_ref[...].astype(BF16), w_ref[...].astype(BF16))

    @pl.when(k == pl.num_programs(1) - 1)
    def _():
        y = _layer_norm_rows(ALPHA * res_ref[...] + acc_ref[...], g_ref[...], b_ref[...])
        y_ref[...] = y
        yb_ref[...] = y.astype(BF16)


def _down_residual_ln(h, w, res, g, b):
    m, kdim = h.shape
    n = w.shape[1]
    tk = min(FF_TILE, kdim)
    grid = (m // LN_ROW_TILE, kdim // tk)
    row = lambda i, k: (i, 0)
    return pl.pallas_call(
        _down_ln_kernel, grid=grid,
        in_specs=[pl.BlockSpec((LN_ROW_TILE, tk), lambda i, k: (i, k)),
                  pl.BlockSpec((tk, n), lambda i, k: (k, 0)),
                  pl.BlockSpec((LN_ROW_TILE, n), row),
                  pl.BlockSpec((1, n), lambda i, k: (0, 0)),
                  pl.BlockSpec((1, n), lambda i, k: (0, 0))],
        out_specs=[pl.BlockSpec((LN_ROW_TILE, n), row), pl.BlockSpec((LN_ROW_TILE, n), row)],
        out_shape=[jax.ShapeDtypeStruct((m, n), F32), jax.ShapeDtypeStruct((m, n), BF16)],
        scratch_shapes=[pltpu.VMEM((LN_ROW_TILE, n), F32)],
        compiler_params=_cparams(("parallel", "arbitrary")), name="down_residual_ln",
    )(h, w, res, g.reshape(1, n), b.reshape(1, n))


def _residual_ln_kernel(res_ref, h_ref, g_ref, b_ref, y_ref, yb_ref):
    y = _layer_norm_rows(ALPHA * res_ref[...] + h_ref[...], g_ref[...], b_ref[...])
    y_ref[...] = y
    yb_ref[...] = y.astype(BF16)


def _residual_ln(res, h, g, b):
    m, n = res.shape
    row = pl.BlockSpec((LN_ROW_TILE, n), lambda i: (i, 0))
    vec = pl.BlockSpec((1, n), lambda i: (0, 0))
    return pl.pallas_call(
        _residual_ln_kernel, grid=(m // LN_ROW_TILE,), in_specs=[row, row, vec, vec],
        out_specs=[row, row],
        out_shape=[jax.ShapeDtypeStruct((m, n), F32), jax.ShapeDtypeStruct((m, n), BF16)],
        compiler_params=_cparams(("parallel",)), name="residual_ln",
    )(res, h, g.reshape(1, n), b.reshape(1, n))


def _router_kernel(x_ref, w_ref, idx_ref, gate_ref):
    logits = _dot_f32(x_ref[...], w_ref[...])
    lane = lax.broadcasted_iota(jnp.int32, logits.shape, 1)
    logits = jnp.where(lane < N_EXPERTS, logits, -jnp.inf)
    v1 = jnp.max(logits, axis=1, keepdims=True)
    i1 = jnp.min(jnp.where(logits == v1, lane, LANES), axis=1, keepdims=True)
    rest = jnp.where(lane == i1, -jnp.inf, logits)
    v2 = jnp.max(rest, axis=1, keepdims=True)
    i2 = jnp.min(jnp.where(rest == v2, lane, LANES), axis=1, keepdims=True)
    e = jnp.exp(v2 - v1)
    g1 = 1.0 / (1.0 + e)
    g2 = e / (1.0 + e)
    idx_ref[...] = jnp.where(lane == 0, i1, jnp.where(lane == 1, i2, 0))
    gate_ref[...] = jnp.where(lane == 0, g1, jnp.where(lane == 1, g2, 0.0))


def _route(x, w_router):
    m, kdim = x.shape
    w_pad = jnp.zeros((kdim, LANES), F32).at[:, :N_EXPERTS].set(w_router)
    out = pl.BlockSpec((LN_ROW_TILE, LANES), lambda i: (i, 0))
    return pl.pallas_call(
        _router_kernel, grid=(m // LN_ROW_TILE,),
        in_specs=[pl.BlockSpec((LN_ROW_TILE, kdim), lambda i: (i, 0)),
                  pl.BlockSpec((kdim, LANES), lambda i: (0, 0))],
        out_specs=[out, out],
        out_shape=[jax.ShapeDtypeStruct((m, LANES), jnp.int32), jax.ShapeDtypeStruct((m, LANES), F32)],
        compiler_params=_cparams(("parallel",)), name="router")(x, w_pad)


def _routing_plan(top_idx, n_tiles):
    e_flat = top_idx.reshape(-1)
    onehot = (e_flat[:, None] == jnp.arange(N_EXPERTS, dtype=jnp.int32)[None, :]).astype(jnp.int32)
    counts = jnp.sum(onehot, axis=0)
    rank = jnp.sum((jnp.cumsum(onehot, axis=0) - onehot) * onehot, axis=1)
    tiles = (counts + GROUP_TILE - 1) // GROUP_TILE
    tile_end = jnp.cumsum(tiles)
    tile_start = tile_end - tiles
    dest = (tile_start * GROUP_TILE)[e_flat] + rank
    t = jnp.arange(n_tiles, dtype=jnp.int32)
    used = tile_end[-1]
    t_eff = jnp.minimum(t, used - 1)
    tile_expert = jnp.sum((t_eff[:, None] >= tile_end[None, :]).astype(jnp.int32), axis=1)
    tile_valid = (t < used).astype(jnp.int32)
    return dest.astype(jnp.int32), tile_expert.astype(jnp.int32), t_eff.astype(jnp.int32), tile_valid


def _dispatch_kernel(dest_ref, x_hbm, zero_hbm, xs_hbm, sem, *, tokens_per_step):
    del zero_hbm
    base = pl.program_id(0) * tokens_per_step

    def copy(t, s):
        return pltpu.make_async_copy(
            x_hbm.at[pl.ds(base + t, 1)], xs_hbm.at[pl.ds(dest_ref[(base + t) * TOP_K + s], 1)], sem)

    def start(t, c):
        for s in range(TOP_K):
            copy(t, s).start()
        return c

    def wait(t, c):
        for s in range(TOP_K):
            copy(t, s).wait()
        return c

    lax.fori_loop(0, tokens_per_step, start, 0)
    lax.fori_loop(0, tokens_per_step, wait, 0)


def _dispatch(x, dest, n_tokens, n_rows):
    n = x.shape[1]
    steps = DISPATCH_STEPS
    tokens_per_step = n_tokens // steps
    assert tokens_per_step * steps == n_tokens
    zeros = jnp.zeros((n_rows, n), x.dtype)
    return pl.pallas_call(
        functools.partial(_dispatch_kernel, tokens_per_step=tokens_per_step),
        grid_spec=pltpu.PrefetchScalarGridSpec(
            num_scalar_prefetch=1, grid=(steps,),
            in_specs=[pl.BlockSpec(memory_space=pl.ANY), pl.BlockSpec(memory_space=pl.ANY)],
            out_specs=pl.BlockSpec(memory_space=pl.ANY),
            scratch_shapes=[pltpu.SemaphoreType.DMA(())]),
        out_shape=jax.ShapeDtypeStruct((n_rows, n), x.dtype),
        input_output_aliases={2: 0},
        compiler_params=_cparams(("arbitrary",)), name="moe_dispatch")(dest, x, zeros)


def _moe_gateup_kernel(te_ref, tr_ref, tv_ref, x_ref, wg_ref, wu_ref, h_ref):
    del te_ref, tr_ref
    i = pl.program_id(1)

    @pl.when(tv_ref[i] == 1)
    def _():
        x = x_ref[...].astype(BF16)
        a = _dot(x, wg_ref[...].astype(BF16))
        u = _dot(x, wu_ref[...].astype(BF16))
        h_ref[...] = (a * (1.0 / (1.0 + jnp.exp(-a))) * u).astype(h_ref.dtype)

    @pl.when(tv_ref[i] == 0)
    def _():
        h_ref[...] = jnp.zeros_like(h_ref)


def _moe_gate_up(xs, wg, wu, plan, n_tiles):
    _, tile_expert, tile_row, tile_valid = plan
    kdim = xs.shape[1]
    n = wg.shape[2]
    w_spec = pl.BlockSpec((None, kdim, FF_TILE), lambda j, i, te, tr, tv: (te[i], 0, j))
    return pl.pallas_call(
        _moe_gateup_kernel,
        grid_spec=pltpu.PrefetchScalarGridSpec(
            num_scalar_prefetch=3, grid=(n // FF_TILE, n_tiles),
            in_specs=[pl.BlockSpec((GROUP_TILE, kdim), lambda j, i, te, tr, tv: (tr[i], 0)), w_spec, w_spec],
            out_specs=pl.BlockSpec((GROUP_TILE, FF_TILE), lambda j, i, te, tr, tv: (i, j))),
        out_shape=jax.ShapeDtypeStruct((n_tiles * GROUP_TILE, n), BF16),
        compiler_params=_cparams(("arbitrary", "arbitrary")), name="moe_gate_up",
    )(tile_expert, tile_row, tile_valid, xs, wg, wu)


def _moe_down_kernel(te_ref, tr_ref, tv_ref, h_ref, w_ref, y_ref, acc_ref):
    del te_ref, tr_ref
    i = pl.program_id(0)
    k = pl.program_id(1)

    @pl.when(k == 0)
    def _():
        acc_ref[...] = jnp.zeros_like(acc_ref)

    @pl.when(tv_ref[i] == 1)
    def _():
        acc_ref[...] += _dot(h_ref[...], w_ref[...].astype(BF16))

    @pl.when(k == pl.num_programs(1) - 1)
    def _():
        y_ref[...] = acc_ref[...]


def _moe_down(h, wd, plan, n_tiles):
    _, tile_expert, tile_row, tile_valid = plan
    kdim = h.shape[1]
    n = wd.shape[2]
    tk = min(MOE_DOWN_K_TILE, kdim)
    return pl.pallas_call(
        _moe_down_kernel,
        grid_spec=pltpu.PrefetchScalarGridSpec(
            num_scalar_prefetch=3, grid=(n_tiles, kdim // tk),
            in_specs=[pl.BlockSpec((GROUP_TILE, tk), lambda i, k, te, tr, tv: (tr[i], k)),
                      pl.BlockSpec((None, tk, n),
                                   lambda i, k, te, tr, tv: (te[i], k * tv[i] + (kdim // tk - 1) * (1 - tv[i]), 0))],
            out_specs=pl.BlockSpec((GROUP_TILE, n), lambda i, k, te, tr, tv: (i, 0)),
            scratch_shapes=[pltpu.VMEM((GROUP_TILE, n), F32)]),
        out_shape=jax.ShapeDtypeStruct((n_tiles * GROUP_TILE, n), F32),
        compiler_params=_cparams(("arbitrary", "arbitrary")), name="moe_down",
    )(tile_expert, tile_row, tile_valid, h, wd)


def _combine_ln_kernel(dest_ref, ys_hbm, res_ref, gate_ref, g_ref, b_ref, y_ref, yb_ref, buf, sem,
                       *, tokens_per_step, n_tokens):
    base = pl.program_id(0) * tokens_per_step

    def copy(t, s):
        tok = jnp.minimum(base + t, n_tokens - 1)
        return pltpu.make_async_copy(
            ys_hbm.at[pl.ds(dest_ref[tok * TOP_K + s], 1)], buf.at[s, pl.ds(t, 1)], sem)

    def start(t, c):
        for s in range(TOP_K):
            copy(t, s).start()
        return c

    def wait(t, c):
        for s in range(TOP_K):
            copy(t, s).wait()
        return c

    lax.fori_loop(0, tokens_per_step, start, 0)
    lax.fori_loop(0, tokens_per_step, wait, 0)
    gate = gate_ref[...]
    f = gate[:, 0:1] * buf[0] + gate[:, 1:2] * buf[1]
    y = _layer_norm_rows(ALPHA * res_ref[...] + f, g_ref[...], b_ref[...])
    y_ref[...] = y
    yb_ref[...] = y.astype(BF16)


def _combine_residual_ln(ys, dest, gates, res, g, b, n_tokens):
    m, n = res.shape
    tps = COMBINE_TILE
    row = lambda i, d: (i, 0)
    vec = pl.BlockSpec((1, n), lambda i, d: (0, 0))
    return pl.pallas_call(
        functools.partial(_combine_ln_kernel, tokens_per_step=tps, n_tokens=n_tokens),
        grid_spec=pltpu.PrefetchScalarGridSpec(
            num_scalar_prefetch=1, grid=(m // tps,),
            in_specs=[pl.BlockSpec(memory_space=pl.ANY),
                      pl.BlockSpec((tps, n), row),
                      pl.BlockSpec((tps, LANES), row), vec, vec],
            out_specs=[pl.BlockSpec((tps, n), row), pl.BlockSpec((tps, n), row)],
            scratch_shapes=[pltpu.VMEM((TOP_K, tps, n), F32), pltpu.SemaphoreType.DMA(())]),
        out_shape=[jax.ShapeDtypeStruct((m, n), F32), jax.ShapeDtypeStruct((m, n), BF16)],
        compiler_params=_cparams(("arbitrary",)), name="moe_combine_ln",
    )(dest, ys, res, gates, g.reshape(1, n), b.reshape(1, n))


def _moe_block(x, xb, n_tokens, w_router, wg, wu, wd, g, b):
    del xb
    top_idx, gates = _route(x, w_router)
    n_rows = n_tokens * TOP_K
    n_tiles = -(-n_rows // GROUP_TILE) + N_EXPERTS
    plan = _routing_plan(top_idx[:n_tokens, :TOP_K], n_tiles)
    xs = _dispatch(x, plan[0], n_tokens, n_tiles * GROUP_TILE)
    h = _moe_gate_up(xs, wg, wu, plan, n_tiles)
    ys = _moe_down(h, wd, plan, n_tiles)
    return _combine_residual_ln(ys, plan[0], gates, x, g, b, n_tokens)


def _stack_heads(q):
    return jnp.concatenate([q[:, h * HEAD_DIM:(h + 1) * HEAD_DIM] for h in range(KV_GROUP)], axis=0)


def _store_heads(o_ref, o):
    blk = o.shape[0] // KV_GROUP
    for h in range(KV_GROUP):
        o_ref[:, h * HEAD_DIM:(h + 1) * HEAD_DIM] = o[h * blk:(h + 1) * blk].astype(o_ref.dtype)


def _topk_lane_mask(gate, n_cand, n_valid):
    lane = lax.broadcasted_iota(jnp.int32, gate.shape, 1)
    cnt = jnp.zeros(gate.shape, jnp.int32)
    for c in range(n_cand):
        col = gate[:, c:c + 1]
        beats = (col > gate) | ((col == gate) & (c < lane))
        cnt = cnt + jnp.where(beats, jnp.where(c < n_valid, 1, 0), 0)
    return (lane < n_valid) & (cnt < MOBA_TOPK)


def _moba_prompt_kernel(q_ref, k_ref, v_ref, o_ref, kmean_ref, *, n_blk):
    qi = pl.program_id(2)
    blk = ATT_BLOCK

    @pl.when(qi == 0)
    def _():
        kmean_ref[...] = jnp.zeros_like(kmean_ref)
        for j in range(n_blk):
            kmean_ref[j:j + 1, :] = jnp.mean(k_ref[j * blk:(j + 1) * blk, :], axis=0, keepdims=True)

    qr = _stack_heads(q_ref[...])
    rows = qr.shape[0]
    gate = _dot_f32(qr, kmean_ref[...], nt=True)
    sel = _topk_lane_mask(gate, n_blk - 1, qi).astype(F32)
    qb = qr.astype(BF16)
    lane_blk = lax.broadcasted_iota(jnp.int32, sel.shape, 1)

    def scores(j):
        kj = k_ref[pl.ds(pl.multiple_of(j * blk, blk), blk), :].astype(BF16)
        return _dot_nt(qb, kj) * ATTN_SCALE

    def values(j):
        return v_ref[pl.ds(pl.multiple_of(j * blk, blk), blk), :].astype(BF16)

    s = scores(qi)
    q_pos = lax.broadcasted_iota(jnp.int32, s.shape, 0) & (blk - 1)
    k_pos = lax.broadcasted_iota(jnp.int32, s.shape, 1)
    s = jnp.where(k_pos <= q_pos, s, NEG_BIG)
    m0 = jnp.max(s, axis=1, keepdims=True)
    p = jnp.exp(s - m0)
    l0 = jnp.sum(p, axis=1, keepdims=True)
    acc0 = _dot(p.astype(BF16), values(qi))

    def body(j, carry):
        m, l, acc = carry
        picked = jnp.max(jnp.where(lane_blk == j, sel, 0.0), axis=1, keepdims=True) > 0.5
        s = jnp.where(picked, scores(j), NEG_BIG)
        m_new = jnp.maximum(m, jnp.max(s, axis=1, keepdims=True))
        a = jnp.exp(m - m_new)
        p = jnp.where(picked, jnp.exp(s - m_new), 0.0)
        l = a * l + jnp.sum(p, axis=1, keepdims=True)
        acc = a * acc + _dot(p.astype(BF16), values(j))
        return m_new, l, acc

    _, l, acc = lax.fori_loop(0, qi, body, (m0, l0, acc0))
    _store_heads(o_ref, acc / l)
    del rows


def _moba_prompt(q, k, v, batch, seq):
    n_q = seq // ATT_BLOCK
    gw = KV_GROUP * HEAD_DIM
    kv_spec = pl.BlockSpec((seq, HEAD_DIM), lambda b, g, i: (b, g))
    return pl.pallas_call(
        functools.partial(_moba_prompt_kernel, n_blk=n_q),
        grid=(batch, N_KV_HEADS, n_q),
        in_specs=[pl.BlockSpec((ATT_BLOCK, gw), lambda b, g, i: (b * n_q + i, g)), kv_spec, kv_spec],
        out_specs=pl.BlockSpec((ATT_BLOCK, gw), lambda b, g, i: (b * n_q + i, g)),
        out_shape=jax.ShapeDtypeStruct((batch * seq, D_MODEL), BF16),
        scratch_shapes=[pltpu.VMEM((LANES, HEAD_DIM), F32)],
        compiler_params=_cparams(("parallel", "parallel", "arbitrary")), name="moba_prompt")(q, k, v)


def _log_sigmoid_pair(z):
    t = jnp.log1p(jnp.exp(-jnp.abs(z)))
    return jnp.minimum(z, 0.0) - t, -jnp.maximum(z, 0.0) - t


def _suffix_sums(x, upper):
    hi = x.astype(BF16)
    lo = (x - hi.astype(F32)).astype(BF16)
    return _dot(hi, upper) + _dot(lo, upper)


def _strict_upper(n):
    r = lax.broadcasted_iota(jnp.int32, (n, n), 0)
    c = lax.broadcasted_iota(jnp.int32, (n, n), 1)
    return jnp.where(r > c, 1.0, 0.0).astype(BF16)


def _sb_prompt_kernel(q_ref, k_ref, v_ref, o_ref):
    qi = pl.program_id(2)
    blk = ATT_BLOCK
    qb = _stack_heads(q_ref[...]).astype(BF16)
    upper = _strict_upper(blk)

    def block(j, carry, acc, causal):
        kj = k_ref[pl.ds(pl.multiple_of(j * blk, blk), blk), :].astype(BF16)
        vj = v_ref[pl.ds(pl.multiple_of(j * blk, blk), blk), :].astype(BF16)
        z = _dot_nt(qb, kj) * ATTN_SCALE
        log_beta, log_keep = _log_sigmoid_pair(z)
        if causal is not None:
            log_keep = jnp.where(causal, log_keep, 0.0)
        w = jnp.exp(log_beta + _suffix_sums(log_keep, upper) + carry)
        if causal is not None:
            w = jnp.where(causal, w, 0.0)
        acc = acc + _dot(w.astype(BF16), vj)
        return carry + jnp.sum(log_keep, axis=1, keepdims=True), acc

    rows = qb.shape[0]
    q_pos = lax.broadcasted_iota(jnp.int32, (rows, blk), 0) & (blk - 1)
    k_pos = lax.broadcasted_iota(jnp.int32, (rows, blk), 1)
    carry, acc = block(qi, jnp.zeros((rows, 1), F32), jnp.zeros((rows, HEAD_DIM), F32), k_pos < q_pos)

    def body(t, c):
        return block(qi - 1 - t, c[0], c[1], None)

    _, acc = lax.fori_loop(0, qi, body, (carry, acc))
    _store_heads(o_ref, acc)


def _sb_prompt(q, k, v, batch, seq):
    n_q = seq // ATT_BLOCK
    gw = KV_GROUP * HEAD_DIM
    kv_spec = pl.BlockSpec((seq, HEAD_DIM), lambda b, g, i: (b, g))
    return pl.pallas_call(
        _sb_prompt_kernel, grid=(batch, N_KV_HEADS, n_q),
        in_specs=[pl.BlockSpec((ATT_BLOCK, gw), lambda b, g, i: (b * n_q + i, g)), kv_spec, kv_spec],
        out_specs=pl.BlockSpec((ATT_BLOCK, gw), lambda b, g, i: (b * n_q + i, g)),
        out_shape=jax.ShapeDtypeStruct((batch * seq, D_MODEL), BF16),
        compiler_params=_cparams(("parallel", "parallel", "arbitrary")), name="sb_prompt")(q, k, v)


def _group_rows(q, dec_batch, dec_seq):
    q = q.reshape(dec_batch, dec_seq, N_KV_HEADS, KV_GROUP, HEAD_DIM)
    return q.transpose(0, 2, 1, 3, 4).reshape(dec_batch, N_KV_HEADS, dec_seq * KV_GROUP, HEAD_DIM)


def _ungroup_rows(o, dec_batch, dec_seq):
    o = o.reshape(dec_batch, N_KV_HEADS, dec_seq, KV_GROUP, HEAD_DIM)
    return o.transpose(0, 2, 1, 3, 4).reshape(dec_batch * dec_seq, D_MODEL)


def _pad_new(x, dec_batch, dec_seq):
    x = x.reshape(dec_batch, dec_seq, KV_WIDTH)
    return jnp.pad(x, ((0, 0), (0, PAGE_SIZE - dec_seq), (0, 0)))


def _page_pair(ref0, ref1, g):
    return jnp.concatenate([ref0[:, g, :], ref1[:, g, :]], axis=0).astype(BF16)


def _moba_sample_kernel(pt_ref, q_ref, kn_ref, vn_ref, k0_ref, k1_ref, v0_ref, v1_ref, o_ref,
                        s_ref, p_ref, kmean_ref, acc_ref, l_ref, *, n_blk, dec_seq):
    del pt_ref
    step = pl.program_id(1)
    rg = dec_seq * KV_GROUP
    blk = 2 * PAGE_SIZE

    @pl.when(step == 0)
    def _():
        kmean_ref[...] = jnp.zeros_like(kmean_ref)

    @pl.when(step < n_blk)
    def _():
        for g in range(N_KV_HEADS):
            kf = jnp.concatenate([k0_ref[:, g, :], k1_ref[:, g, :]], axis=0)
            kmean_ref[g, pl.ds(step, 1), :] = jnp.mean(kf, axis=0, keepdims=True)
            s_ref[step, g * rg:(g + 1) * rg, :] = _dot_nt(q_ref[g].astype(BF16), kf.astype(BF16)) * ATTN_SCALE

    @pl.when(step == n_blk)
    def _():
        sel_g, s_new_g = [], []
        tok = lax.broadcasted_iota(jnp.int32, (rg, PAGE_SIZE), 0) // KV_GROUP
        key = lax.broadcasted_iota(jnp.int32, (rg, PAGE_SIZE), 1)
        for g in range(N_KV_HEADS):
            gate = _dot_f32(q_ref[g], kmean_ref[g], nt=True)
            sel_g.append(_topk_lane_mask(gate, n_blk, n_blk).astype(F32))
            kn = kn_ref[:, g * HEAD_DIM:(g + 1) * HEAD_DIM].astype(BF16)
            s_new = _dot_nt(q_ref[g].astype(BF16), kn) * ATTN_SCALE
            s_new_g.append(jnp.where(key <= tok, s_new, NEG_BIG))
        sel = jnp.concatenate(sel_g, axis=0)
        s_new = jnp.concatenate(s_new_g, axis=0)
        m = jnp.max(s_new, axis=1, keepdims=True)
        for j in range(n_blk):
            picked = sel[:, j:j + 1] > 0.5
            m = jnp.maximum(m, jnp.max(jnp.where(picked, s_ref[j], NEG_BIG), axis=1, keepdims=True))
        p_new = jnp.exp(s_new - m)
        l = jnp.sum(p_new, axis=1, keepdims=True)
        for j in range(n_blk):
            picked = sel[:, j:j + 1] > 0.5
            p = jnp.where(picked, jnp.exp(s_ref[j] - m), 0.0)
            l = l + jnp.sum(p, axis=1, keepdims=True)
            p_ref[j] = p.astype(BF16)
        l_ref[...] = l
        for g in range(N_KV_HEADS):
            vn = vn_ref[:, g * HEAD_DIM:(g + 1) * HEAD_DIM].astype(BF16)
            acc_ref[g * rg:(g + 1) * rg, :] = _dot(p_new[g * rg:(g + 1) * rg].astype(BF16), vn)

    @pl.when(step >= n_blk)
    def _():
        j = step - n_blk
        for g in range(N_KV_HEADS):
            acc_ref[g * rg:(g + 1) * rg, :] += _dot(p_ref[j, g * rg:(g + 1) * rg, :], _page_pair(v0_ref, v1_ref, g))

    @pl.when(step == 2 * n_blk - 1)
    def _():
        out = acc_ref[...] / l_ref[...]
        for g in range(N_KV_HEADS):
            o_ref[g] = out[g * rg:(g + 1) * rg]
    del blk


def _moba_sample(qg, k_new, v_new, cache_k, cache_v, page_table, dec_seq):
    dec_batch, n_pages = page_table.shape
    n_blk = n_pages // 2
    rg = dec_seq * KV_GROUP
    rows = N_KV_HEADS * rg
    page = (None, PAGE_SIZE, N_KV_HEADS, HEAD_DIM)

    def k_map(off):
        return lambda b, s, pt: (pt[b, 2 * jnp.minimum(s, n_blk - 1) + off], 0, 0, 0)

    def v_map(off):
        return lambda b, s, pt: (pt[b, 2 * jnp.maximum(s - n_blk, 0) + off], 0, 0, 0)

    per_b4 = pl.BlockSpec((None, N_KV_HEADS, rg, HEAD_DIM), lambda b, s, pt: (b, 0, 0, 0))
    per_b3 = pl.BlockSpec((None, PAGE_SIZE, KV_WIDTH), lambda b, s, pt: (b, 0, 0))
    return pl.pallas_call(
        functools.partial(_moba_sample_kernel, n_blk=n_blk, dec_seq=dec_seq),
        grid_spec=pltpu.PrefetchScalarGridSpec(
            num_scalar_prefetch=1, grid=(dec_batch, 2 * n_blk),
            in_specs=[per_b4, per_b3, per_b3,
                      pl.BlockSpec(page, k_map(0)), pl.BlockSpec(page, k_map(1)),
                      pl.BlockSpec(page, v_map(0)), pl.BlockSpec(page, v_map(1))],
            out_specs=per_b4,
            scratch_shapes=[pltpu.VMEM((n_blk, rows, 2 * PAGE_SIZE), F32),
                            pltpu.VMEM((n_blk, rows, 2 * PAGE_SIZE), BF16),
                            pltpu.VMEM((N_KV_HEADS, LANES, HEAD_DIM), F32),
                            pltpu.VMEM((rows, HEAD_DIM), F32),
                            pltpu.VMEM((rows, 1), F32)]),
        out_shape=jax.ShapeDtypeStruct((dec_batch, N_KV_HEADS, rg, HEAD_DIM), F32),
        compiler_params=_cparams(("parallel", "arbitrary")), name="moba_sample",
    )(page_table, qg, k_new, v_new, cache_k, cache_k, cache_v, cache_v)


def _sb_sample_kernel(pt_ref, q_ref, kn_ref, vn_ref, k0_ref, k1_ref, v0_ref, v1_ref, o_ref,
                      acc_ref, carry_ref, *, dec_seq):
    del pt_ref
    step = pl.program_id(1)
    rg = dec_seq * KV_GROUP
    blk = 2 * PAGE_SIZE

    def accumulate(g, kb, vb, upper, causal):
        rows = slice(g * rg, (g + 1) * rg)
        z = _dot_nt(q_ref[g].astype(BF16), kb) * ATTN_SCALE
        log_beta, log_keep = _log_sigmoid_pair(z)
        if causal is not None:
            log_keep = jnp.where(causal, log_keep, 0.0)
        w = jnp.exp(log_beta + _suffix_sums(log_keep, upper) + carry_ref[rows, :])
        if causal is not None:
            w = jnp.where(causal, w, 0.0)
        acc_ref[rows, :] += _dot(w.astype(BF16), vb)
        carry_ref[rows, :] += jnp.sum(log_keep, axis=1, keepdims=True)

    @pl.when(step == 0)
    def _():
        acc_ref[...] = jnp.zeros_like(acc_ref)
        carry_ref[...] = jnp.zeros_like(carry_ref)
        tok = lax.broadcasted_iota(jnp.int32, (rg, PAGE_SIZE), 0) // KV_GROUP
        key = lax.broadcasted_iota(jnp.int32, (rg, PAGE_SIZE), 1)
        upper = _strict_upper(PAGE_SIZE)
        for g in range(N_KV_HEADS):
            cols = slice(g * HEAD_DIM, (g + 1) * HEAD_DIM)
            accumulate(g, kn_ref[:, cols].astype(BF16), vn_ref[:, cols].astype(BF16), upper, key < tok)

    upper = _strict_upper(blk)
    for g in range(N_KV_HEADS):
        accumulate(g, _page_pair(k0_ref, k1_ref, g), _page_pair(v0_ref, v1_ref, g), upper, None)

    @pl.when(step == pl.num_programs(1) - 1)
    def _():
        for g in range(N_KV_HEADS):
            o_ref[g] = acc_ref[g * rg:(g + 1) * rg, :]


def _sb_sample(qg, k_new, v_new, cache_k, cache_v, page_table, dec_seq):
    dec_batch, n_pages = page_table.shape
    n_blk = n_pages // 2
    rg = dec_seq * KV_GROUP
    rows = N_KV_HEADS * rg
    page = (None, PAGE_SIZE, N_KV_HEADS, HEAD_DIM)

    def page_map(off):
        return lambda b, s, pt: (pt[b, 2 * (n_blk - 1 - s) + off], 0, 0, 0)

    per_b4 = pl.BlockSpec((None, N_KV_HEADS, rg, HEAD_DIM), lambda b, s, pt: (b, 0, 0, 0))
    per_b3 = pl.BlockSpec((None, PAGE_SIZE, KV_WIDTH), lambda b, s, pt: (b, 0, 0))
    return pl.pallas_call(
        functools.partial(_sb_sample_kernel, dec_seq=dec_seq),
        grid_spec=pltpu.PrefetchScalarGridSpec(
            num_scalar_prefetch=1, grid=(dec_batch, n_blk),
            in_specs=[per_b4, per_b3, per_b3,
                      pl.BlockSpec(page, page_map(0)), pl.BlockSpec(page, page_map(1)),
                      pl.BlockSpec(page, page_map(0)), pl.BlockSpec(page, page_map(1))],
            out_specs=per_b4,
            scratch_shapes=[pltpu.VMEM((rows, HEAD_DIM), F32), pltpu.VMEM((rows, 1), F32)]),
        out_shape=jax.ShapeDtypeStruct((dec_batch, N_KV_HEADS, rg, HEAD_DIM), F32),
        compiler_params=_cparams(("parallel", "arbitrary")), name="sb_sample",
    )(page_table, qg, k_new, v_new, cache_k, cache_k, cache_v, cache_v)


def _pool_kernel(x_ref, w_ref, scale_ref, o_ref, *, seq, chunk, counts):
    g = pl.program_id(1)
    halo = POOL_STATE + 1
    w = w_ref[...].astype(BF16)
    for c0 in range(0, seq, chunk):
        cur = x_ref[halo + c0:halo + c0 + chunk, :]
        tot, out = cur, None
        for d in range(1, max(POOL_WINDOWS)):
            tot = tot + x_ref[halo + c0 - d:halo + c0 - d + chunk, :]
            if d + 1 in POOL_WINDOWS:
                gi = POOL_WINDOWS.index(d + 1)
                mixed = tot / counts(gi, c0, chunk) - cur
                out = mixed if out is None else jnp.where(g == gi, mixed, out)
        o_ref[c0:c0 + chunk, :] = _dot(out.astype(BF16), w) * scale_ref[...]


def _pool_mixer(x_hist, w_pool, scale, seq, first_pos):
    batch = x_hist.shape[0]
    halo = POOL_STATE + 1
    chunk = min(seq, 512)

    def counts(gi, c0, rows):
        win = POOL_WINDOWS[gi]
        pos = first_pos + c0 + lax.broadcasted_iota(jnp.int32, (rows, 1), 0)
        return jnp.minimum(win, pos + 1).astype(F32)

    return pl.pallas_call(
        functools.partial(_pool_kernel, seq=seq, chunk=chunk, counts=counts),
        grid=(batch, len(POOL_WINDOWS)),
        in_specs=[pl.BlockSpec((None, halo + seq, POOL_GROUP), lambda b, g: (b, 0, g)),
                  pl.BlockSpec((None, POOL_GROUP, POOL_GROUP), lambda b, g: (g, 0, 0)),
                  pl.BlockSpec((1, POOL_GROUP), lambda b, g: (0, g))],
        out_specs=pl.BlockSpec((None, seq, POOL_GROUP), lambda b, g: (b, 0, g)),
        out_shape=jax.ShapeDtypeStruct((batch, seq, D_MODEL), F32),
        compiler_params=_cparams(("parallel", "parallel")), name="pool_mixer",
    )(x_hist, w_pool, scale.reshape(1, D_MODEL))


def _rope_tables(pos):
    half = ROT_DIM // 2
    inv = ROPE_THETA ** (-jnp.arange(0, ROT_DIM, 2, dtype=F32) / ROT_DIM)
    ang = pos.astype(F32)[:, None] * inv[None, :]
    cos, sin = jnp.cos(ang), jnp.sin(ang)
    n = pos.shape[0]
    ones = jnp.ones((n, HEAD_DIM - ROT_DIM), F32)
    zeros = jnp.zeros((n, HEAD_DIM - half), F32)
    a = jnp.concatenate([cos, cos, ones], axis=1)
    b = jnp.concatenate([-sin, zeros], axis=1)
    c = jnp.concatenate([jnp.zeros((n, half), F32), sin, jnp.zeros((n, HEAD_DIM - ROT_DIM), F32)], axis=1)
    return a, b, c


def kernel(x_prompt, x_sample, cache_k_l0, cache_v_l0, state_pool_l1, cache_k_l2, cache_v_l2, cache_k_l3, cache_v_l3, page_table, l0_wq, l0_wk, l0_wv, l0_wo, l0_ln1_g, l0_ln1_b, l0_ffn_wg, l0_ffn_wu, l0_ffn_wd, l0_ln2_g, l0_ln2_b, l1_pool_w, l1_pool_scale, l1_ln1_g, l1_ln1_b, l1_router, l1_moe_wg, l1_moe_wu, l1_moe_wd, l1_ln2_g, l1_ln2_b, l2_wq, l2_wk, l2_wv, l2_wo, l2_ln1_g, l2_ln1_b, l2_ffn_wg, l2_ffn_wu, l2_ffn_wd, l2_ln2_g, l2_ln2_b, l3_wq, l3_wk, l3_wv, l3_wo, l3_ln1_g, l3_ln1_b, l3_router, l3_moe_wg, l3_moe_wu, l3_moe_wd, l3_ln2_g, l3_ln2_b):
    batch, seq, _ = x_prompt.shape
    dec_batch, dec_seq, _ = x_sample.shape
    n_prompt = batch * seq
    n_sample = dec_batch * dec_seq
    n_tokens = n_prompt + n_sample
    m_pad = -(-n_tokens // ROW_TILE) * ROW_TILE
    assert m_pad % LN_ROW_TILE == 0 and seq % ATT_BLOCK == 0 and PAST_LEN % MOBA_BLOCK == 0
    assert PAST_LEN // MOBA_BLOCK == (PAST_LEN + dec_seq - 1) // MOBA_BLOCK and dec_seq <= PAGE_SIZE

    x = jnp.concatenate([x_prompt.reshape(n_prompt, D_MODEL), x_sample.reshape(n_sample, D_MODEL),
                         jnp.zeros((m_pad - n_tokens, D_MODEL), F32)], axis=0)
    xb = x.astype(BF16)
    pos = jnp.concatenate([jnp.tile(jnp.arange(seq, dtype=jnp.int32), batch),
                           jnp.tile(PAST_LEN + jnp.arange(dec_seq, dtype=jnp.int32), dec_batch),
                           jnp.zeros((m_pad - n_tokens,), jnp.int32)])
    rope = _rope_tables(pos)

    def split(y, width):
        p = y[:n_prompt].reshape(batch, seq, width)
        s = y[n_prompt:n_tokens].reshape(dec_batch, dec_seq, width)
        return p, s

    def attention_layer(kind, x, xb, w, caches):
        wq, wk, wv, wo = w
        q = _project(xb, wq, rope if kind == 0 else None)
        k = _project(xb, wk, rope if kind == 0 else None)
        v = _project(xb, wv)
        prompt_fn = _moba_prompt if kind == 0 else _sb_prompt
        sample_fn = _moba_sample if kind == 0 else _sb_sample
        o_p = prompt_fn(q, k, v, batch, seq)
        qg = _group_rows(q[n_prompt:n_tokens], dec_batch, dec_seq)
        o_s = sample_fn(qg, _pad_new(k[n_prompt:n_tokens], dec_batch, dec_seq),
                        _pad_new(v[n_prompt:n_tokens], dec_batch, dec_seq), caches[0], caches[1], page_table, dec_seq)
        o_s = _ungroup_rows(o_s, dec_batch, dec_seq).astype(BF16)
        o = jnp.concatenate([o_p, o_s, jnp.zeros((m_pad - n_tokens, D_MODEL), BF16)], axis=0)
        kp, ks = split(k, KV_WIDTH)
        vp, vs = split(v, KV_WIDTH)
        shape4 = lambda t: t.reshape(t.shape[0], t.shape[1], N_KV_HEADS, HEAD_DIM)
        return o, wo, (shape4(kp), shape4(vp), shape4(ks), shape4(vs))

    def pool_layer(x):
        xp, xs = split(x, D_MODEL)
        hist_p = jnp.concatenate([jnp.zeros((batch, POOL_STATE + 1, D_MODEL), F32), xp], axis=1)
        seq_s = -(-dec_seq // SUBLANES) * SUBLANES
        hist_s = jnp.concatenate([jnp.zeros((dec_batch, 1, D_MODEL), F32), state_pool_l1, xs,
                                  jnp.zeros((dec_batch, seq_s - dec_seq, D_MODEL), F32)], axis=1)
        h_p = _pool_mixer(hist_p, l1_pool_w, l1_pool_scale, seq, 0)
        h_s = _pool_mixer(hist_s, l1_pool_w, l1_pool_scale, seq_s, PAST_LEN)[:, :dec_seq]
        h = jnp.concatenate([h_p.reshape(n_prompt, D_MODEL), h_s.reshape(n_sample, D_MODEL),
                             jnp.zeros((m_pad - n_tokens, D_MODEL), F32)], axis=0)
        state_p = hist_p[:, -POOL_STATE:]
        state_s = jnp.concatenate([state_pool_l1, xs], axis=1)[:, -POOL_STATE:]
        return h, (state_p, state_s)

    def dense_ffn(x, xb, w, g, b):
        wg, wu, wd = w
        return _down_residual_ln(_gate_up(xb, wg, wu), wd, x, g, b)

    o, wo, kv0 = attention_layer(0, x, xb, (l0_wq, l0_wk, l0_wv, l0_wo), (cache_k_l0, cache_v_l0))
    x, xb = _down_residual_ln(o, wo, x, l0_ln1_g, l0_ln1_b)
    x, xb = dense_ffn(x, xb, (l0_ffn_wg, l0_ffn_wu, l0_ffn_wd), l0_ln2_g, l0_ln2_b)
    h, pool_state = pool_layer(x)
    x, xb = _residual_ln(x, h, l1_ln1_g, l1_ln1_b)
    x, xb = _moe_block(x, xb, n_tokens, l1_router, l1_moe_wg, l1_moe_wu, l1_moe_wd, l1_ln2_g, l1_ln2_b)
    o, wo, kv2 = attention_layer(2, x, xb, (l2_wq, l2_wk, l2_wv, l2_wo), (cache_k_l2, cache_v_l2))
    x, xb = _down_residual_ln(o, wo, x, l2_ln1_g, l2_ln1_b)
    x, xb = dense_ffn(x, xb, (l2_ffn_wg, l2_ffn_wu, l2_ffn_wd), l2_ln2_g, l2_ln2_b)
    o, wo, kv3 = attention_layer(0, x, xb, (l3_wq, l3_wk, l3_wv, l3_wo), (cache_k_l3, cache_v_l3))
    x, xb = _down_residual_ln(o, wo, x, l3_ln1_g, l3_ln1_b)
    x, xb = _moe_block(x, xb, n_tokens, l3_router, l3_moe_wg, l3_moe_wu, l3_moe_wd, l3_ln2_g, l3_ln2_b)

    y_prompt, y_sample = split(x, D_MODEL)
    return (y_prompt, y_sample, kv0[0], kv0[1], kv0[2], kv0[3], pool_state[0], pool_state[1],
            kv2[0], kv2[1], kv2[2], kv2[3], kv3[0], kv3[1], kv3[2], kv3[3])
```

```python
import functools
import math

import jax
import jax.numpy as jnp
from jax import lax
from jax.experimental import pallas as pl
from jax.experimental.pallas import tpu as pltpu

F32 = jnp.float32
BF16 = jnp.bfloat16

D_MODEL = 2048
DEPTH = 4
PAST_LEN = 16384
PAGE_SIZE = 128
N_HEADS = 16
N_KV_HEADS = 4
HEAD_DIM = D_MODEL // N_HEADS
KV_GROUP = N_HEADS // N_KV_HEADS
KV_WIDTH = N_KV_HEADS * HEAD_DIM
ROT_DIM = HEAD_DIM // 4
ROPE_THETA = 500000.0
MOBA_BLOCK = 256
MOBA_TOPK = 3
POOL_WINDOWS = (2, 4, 8, 16)
POOL_GROUP = D_MODEL // len(POOL_WINDOWS)
POOL_STATE = max(POOL_WINDOWS) - 1
N_EXPERTS = 8
TOP_K = 2
ALPHA = (2 * DEPTH) ** 0.25
LN_EPS = 1e-5
ATTN_SCALE = HEAD_DIM ** -0.5

LANES = 128
SUBLANES = 8
VMEM_LIMIT_BYTES = 56 * 1024 * 1024

ROW_TILE = 1056
LN_ROW_TILE = 528
FF_TILE = 512
GROUP_TILE = 512
MOE_DOWN_K_TILE = 1024
COMBINE_TILE = 264
DOWN_ROW_TILE = 1056
DOWN_K_TILE = 256
LN_CHUNK = 96
ROW_CHUNK = 64
PAGES_PER_STEP = 8
ATT_BLOCK = 256
NEG_BIG = -1e30
EXP_ZERO_BELOW = -104.0


def _cparams(sem):
    return pltpu.CompilerParams(dimension_semantics=sem, vmem_limit_bytes=VMEM_LIMIT_BYTES)


def _dot(a, b):
    return jnp.dot(a, b, preferred_element_type=F32)


def _dot_nt(a, b):
    return lax.dot_general(a, b, (((1,), (1,)), ((), ())), preferred_element_type=F32)


def _split_bf16(x, n):
    parts, r = [], x
    for _ in range(n):
        p = r.astype(BF16)
        parts.append(p)
        r = r - p.astype(F32)
    return parts


def _dot_f32(a, b, nt=False):
    a3, b3 = _split_bf16(a, 3), _split_bf16(b, 3)
    dot = _dot_nt if nt else _dot
    out = None
    for i, j in ((0, 2), (1, 1), (2, 0), (0, 1), (1, 0), (0, 0)):
        t = dot(a3[i], b3[j])
        out = t if out is None else out + t
    return out


def _layer_norm_rows(y, g, b):
    mu = jnp.mean(y, axis=-1, keepdims=True)
    d = y - mu
    var = jnp.mean(d * d, axis=-1, keepdims=True)
    return d * lax.rsqrt(var + LN_EPS) * g + b


def _proj_kernel(x_ref, w_ref, o_ref):
    o_ref[...] = _dot(x_ref[...].astype(BF16), w_ref[...].astype(BF16))


def _proj_rope_kernel(x_ref, w_ref, ca_ref, sb_ref, sc_ref, o_ref):
    y = _dot(x_ref[...].astype(BF16), w_ref[...].astype(BF16))
    half = ROT_DIM // 2
    ca, sb, sc = ca_ref[...], sb_ref[...], sc_ref[...]
    for h in range(y.shape[1] // HEAD_DIM):
        yh = y[:, h * HEAD_DIM:(h + 1) * HEAD_DIM]
        up = pltpu.roll(yh, HEAD_DIM - half, axis=1)
        down = pltpu.roll(yh, half, axis=1)
        o_ref[:, h * HEAD_DIM:(h + 1) * HEAD_DIM] = yh * ca + up * sb + down * sc


def _project(x, w, rope=None, col_tile=512):
    m, kdim = x.shape
    n = w.shape[1]
    tn = min(col_tile, n)
    grid = (m // ROW_TILE, n // tn)
    x_spec = pl.BlockSpec((ROW_TILE, kdim), lambda i, j: (i, 0))
    w_spec = pl.BlockSpec((kdim, tn), lambda i, j: (0, j))
    o_spec = pl.BlockSpec((ROW_TILE, tn), lambda i, j: (i, j))
    if rope is None:
        return pl.pallas_call(
            _proj_kernel, grid=grid, in_specs=[x_spec, w_spec], out_specs=o_spec,
            out_shape=jax.ShapeDtypeStruct((m, n), F32),
            compiler_params=_cparams(("parallel", "arbitrary")), name="project")(x, w)
    t_spec = pl.BlockSpec((ROW_TILE, HEAD_DIM), lambda i, j: (i, 0))
    return pl.pallas_call(
        _proj_rope_kernel, grid=grid, in_specs=[x_spec, w_spec, t_spec, t_spec, t_spec],
        out_specs=o_spec, out_shape=jax.ShapeDtypeStruct((m, n), F32),
        compiler_params=_cparams(("parallel", "arbitrary")), name="project_rope")(x, w, *rope)


def _gateup_kernel(x_ref, wg_ref, wu_ref, h_ref):
    x = x_ref[...].astype(BF16)
    a = _dot(x, wg_ref[...].astype(BF16))
    u = _dot(x, wu_ref[...].astype(BF16))
    h_ref[...] = (a * (1.0 / (1.0 + jnp.exp(-a))) * u).astype(h_ref.dtype)


def _gate_up(x, wg, wu):
    m, kdim = x.shape
    n = wg.shape[1]
    grid = (m // ROW_TILE, n // FF_TILE)
    return pl.pallas_call(
        _gateup_kernel, grid=grid,
        in_specs=[pl.BlockSpec((ROW_TILE, kdim), lambda i, j: (i, 0)),
                  pl.BlockSpec((kdim, FF_TILE), lambda i, j: (0, j)),
                  pl.BlockSpec((kdim, FF_TILE), lambda i, j: (0, j))],
        out_specs=pl.BlockSpec((ROW_TILE, FF_TILE), lambda i, j: (i, j)),
        out_shape=jax.ShapeDtypeStruct((m, n), BF16),
        compiler_params=_cparams(("parallel", "arbitrary")), name="gate_up")(x, wg, wu)


def _down_ln_kernel(h_ref, w_ref, res_ref, g_ref, b_ref, y_ref, yb_ref, acc_ref):
    k = pl.program_id(1)

    @pl.when(k == 0)
    def _():
        acc_ref[...] = jnp.zeros_like(acc_ref)

    acc_ref[...] += _dot(h_ref[...].astype(BF16), w_ref[...].astype(BF16))

    @pl.when(k == pl.num_programs(1) - 1)
    def _():
        rows = acc_ref.shape[0]
        chunk = math.gcd(rows, LN_CHUNK)

        def body(c, carry):
            r = pl.ds(pl.multiple_of(c * chunk, 2 * SUBLANES), chunk)
            y = _layer_norm_rows(ALPHA * res_ref[r, :] + acc_ref[r, :], g_ref[...], b_ref[...])
            y_ref[r, :] = y
            yb_ref[r, :] = y.astype(BF16)
            return carry

        lax.fori_loop(0, rows // chunk, body, 0)


def _down_residual_ln(h, w, res, g, b):
    m, kdim = h.shape
    n = w.shape[1]
    tm = DOWN_ROW_TILE
    tk = min(DOWN_K_TILE, kdim)
    grid = (m // tm, kdim // tk)
    row = lambda i, k: (i, 0)
    return pl.pallas_call(
        _down_ln_kernel, grid=grid,
        in_specs=[pl.BlockSpec((tm, tk), lambda i, k: (i, k)),
                  pl.BlockSpec((tk, n), lambda i, k: (k, 0)),
                  pl.BlockSpec((tm, n), row, pipeline_mode=pl.Buffered(1)),
                  pl.BlockSpec((1, n), lambda i, k: (0, 0)),
                  pl.BlockSpec((1, n), lambda i, k: (0, 0))],
        out_specs=[pl.BlockSpec((tm, n), row), pl.BlockSpec((tm, n), row)],
        out_shape=[jax.ShapeDtypeStruct((m, n), F32), jax.ShapeDtypeStruct((m, n), BF16)],
        scratch_shapes=[pltpu.VMEM((tm, n), F32)],
        compiler_params=_cparams(("parallel", "arbitrary")), name="down_residual_ln",
    )(h, w, res, g.reshape(1, n), b.reshape(1, n))


def _residual_ln_kernel(res_ref, h_ref, g_ref, b_ref, y_ref, yb_ref):
    y = _layer_norm_rows(ALPHA * res_ref[...] + h_ref[...], g_ref[...], b_ref[...])
    y_ref[...] = y
    yb_ref[...] = y.astype(BF16)


def _residual_ln(res, h, g, b):
    m, n = res.shape
    row = pl.BlockSpec((LN_ROW_TILE, n), lambda i: (i, 0))
    vec = pl.BlockSpec((1, n), lambda i: (0, 0))
    return pl.pallas_call(
        _residual_ln_kernel, grid=(m // LN_ROW_TILE,), in_specs=[row, row, vec, vec],
        out_specs=[row, row],
        out_shape=[jax.ShapeDtypeStruct((m, n), F32), jax.ShapeDtypeStruct((m, n), BF16)],
        compiler_params=_cparams(("parallel",)), name="residual_ln",
    )(res, h, g.reshape(1, n), b.reshape(1, n))


def _router_kernel(x_ref, w_ref, idx_ref, gate_ref):
    logits = _dot_f32(x_ref[...], w_ref[...])
    lane = lax.broadcasted_iota(jnp.int32, logits.shape, 1)
    logits = jnp.where(lane < N_EXPERTS, logits, -jnp.inf)
    v1 = jnp.max(logits, axis=1, keepdims=True)
    i1 = jnp.min(jnp.where(logits == v1, lane, LANES), axis=1, keepdims=True)
    rest = jnp.where(lane == i1, -jnp.inf, logits)
    v2 = jnp.max(rest, axis=1, keepdims=True)
    i2 = jnp.min(jnp.where(rest == v2, lane, LANES), axis=1, keepdims=True)
    e = jnp.exp(v2 - v1)
    g1 = 1.0 / (1.0 + e)
    g2 = e / (1.0 + e)
    idx_ref[...] = jnp.where(lane == 0, i1, jnp.where(lane == 1, i2, 0))
    gate_ref[...] = jnp.where(lane == 0, g1, jnp.where(lane == 1, g2, 0.0))


def _route(x, w_router):
    m, kdim = x.shape
    w_pad = jnp.zeros((kdim, LANES), F32).at[:, :N_EXPERTS].set(w_router)
    out = pl.BlockSpec((LN_ROW_TILE, LANES), lambda i: (i, 0))
    return pl.pallas_call(
        _router_kernel, grid=(m // LN_ROW_TILE,),
        in_specs=[pl.BlockSpec((LN_ROW_TILE, kdim), lambda i: (i, 0)),
                  pl.BlockSpec((kdim, LANES), lambda i: (0, 0))],
        out_specs=[out, out],
        out_shape=[jax.ShapeDtypeStruct((m, LANES), jnp.int32), jax.ShapeDtypeStruct((m, LANES), F32)],
        compiler_params=_cparams(("parallel",)), name="router")(x, w_pad)


def _routing_plan(top_idx, n_tiles):
    e_flat = top_idx.reshape(-1)
    onehot = (e_flat[:, None] == jnp.arange(N_EXPERTS, dtype=jnp.int32)[None, :]).astype(jnp.int32)
    counts = jnp.sum(onehot, axis=0)
    rank = jnp.sum((jnp.cumsum(onehot, axis=0) - onehot) * onehot, axis=1)
    tiles = (counts + GROUP_TILE - 1) // GROUP_TILE
    tile_end = jnp.cumsum(tiles)
    tile_start = tile_end - tiles
    dest = ((tile_start * GROUP_TILE)[e_flat] + rank).astype(jnp.int32)
    t = jnp.arange(n_tiles, dtype=jnp.int32)
    used = tile_end[-1]
    t_eff = jnp.minimum(t, used - 1)
    tile_expert = jnp.sum((t_eff[:, None] >= tile_end[None, :]).astype(jnp.int32), axis=1)
    tile_valid = (t < used).astype(jnp.int32)
    prev_expert = jnp.concatenate([jnp.full((1,), -1, jnp.int32), tile_expert[:-1].astype(jnp.int32)])
    tile_first = (tile_expert != prev_expert).astype(jnp.int32)
    src = jnp.zeros((n_tiles * GROUP_TILE,), jnp.int32).at[dest].set(
        jnp.arange(e_flat.shape[0], dtype=jnp.int32) // TOP_K, unique_indices=True)
    return dest, tile_expert.astype(jnp.int32), t_eff.astype(jnp.int32), tile_valid, tile_first, src


def _gather_kernel(src_ref, tv_ref, x_hbm, o_ref, buf, sem):
    i = pl.program_id(0)
    rows = buf.shape[0]

    def copy(r):
        return pltpu.make_async_copy(x_hbm.at[pl.ds(src_ref[i * rows + r], 1)], buf.at[pl.ds(r, 1)], sem)

    def start(r, c):
        copy(r).start()
        return c

    def wait(r, c):
        copy(r).wait()
        return c

    @pl.when(tv_ref[i] == 1)
    def _():
        lax.fori_loop(0, rows, start, 0, unroll=8)
        lax.fori_loop(0, rows, wait, 0, unroll=8)
        o_ref[...] = buf[...].astype(o_ref.dtype)

    @pl.when(tv_ref[i] == 0)
    def _():
        o_ref[...] = jnp.zeros_like(o_ref)


def _gather_rows(x, plan, n_tiles):
    n = x.shape[1]
    return pl.pallas_call(
        _gather_kernel,
        grid_spec=pltpu.PrefetchScalarGridSpec(
            num_scalar_prefetch=2, grid=(n_tiles,),
            in_specs=[pl.BlockSpec(memory_space=pl.ANY)],
            out_specs=pl.BlockSpec((GROUP_TILE, n), lambda i, src, tv: (i, 0)),
            scratch_shapes=[pltpu.VMEM((GROUP_TILE, n), x.dtype), pltpu.SemaphoreType.DMA(())]),
        out_shape=jax.ShapeDtypeStruct((n_tiles * GROUP_TILE, n), BF16),
        compiler_params=_cparams(("arbitrary",)), name="moe_gather")(plan[5], plan[3], x)


def _moe_gateup_kernel(te_ref, tr_ref, tv_ref, tf_ref, x_ref, wg_ref, wu_ref, h_ref, wgb_ref, wub_ref):
    del te_ref, tr_ref
    i = pl.program_id(1)

    @pl.when(tf_ref[i] == 1)
    def _():
        wgb_ref[...] = wg_ref[...].astype(BF16)
        wub_ref[...] = wu_ref[...].astype(BF16)

    @pl.when(tv_ref[i] == 1)
    def _():
        x = x_ref[...]
        a = _dot(x, wgb_ref[...])
        u = _dot(x, wub_ref[...])
        h_ref[...] = (a * (1.0 / (1.0 + jnp.exp(-a))) * u).astype(h_ref.dtype)

    @pl.when(tv_ref[i] == 0)
    def _():
        h_ref[...] = jnp.zeros_like(h_ref)


def _moe_gate_up(xs, wg, wu, plan, n_tiles):
    _, tile_expert, tile_row, tile_valid, tile_first, _ = plan
    kdim = xs.shape[1]
    n = wg.shape[2]
    w_spec = pl.BlockSpec((None, kdim, FF_TILE), lambda j, i, te, tr, tv, tf: (te[i], 0, j))
    return pl.pallas_call(
        _moe_gateup_kernel,
        grid_spec=pltpu.PrefetchScalarGridSpec(
            num_scalar_prefetch=4, grid=(n // FF_TILE, n_tiles),
            in_specs=[pl.BlockSpec((GROUP_TILE, kdim), lambda j, i, te, tr, tv, tf: (tr[i], 0)), w_spec, w_spec],
            out_specs=pl.BlockSpec((GROUP_TILE, FF_TILE), lambda j, i, te, tr, tv, tf: (i, j)),
            scratch_shapes=[pltpu.VMEM((kdim, FF_TILE), BF16), pltpu.VMEM((kdim, FF_TILE), BF16)]),
        out_shape=jax.ShapeDtypeStruct((n_tiles * GROUP_TILE, n), BF16),
        compiler_params=_cparams(("arbitrary", "arbitrary")), name="moe_gate_up",
    )(tile_expert, tile_row, tile_valid, tile_first, xs, wg, wu)


def _moe_down_kernel(te_ref, tr_ref, tv_ref, h_ref, w_ref, y_ref, acc_ref):
    del te_ref, tr_ref
    i = pl.program_id(0)
    k = pl.program_id(1)

    @pl.when(k == 0)
    def _():
        acc_ref[...] = jnp.zeros_like(acc_ref)

    @pl.when(tv_ref[i] == 1)
    def _():
        acc_ref[...] += _dot(h_ref[...], w_ref[...].astype(BF16))

    @pl.when(k == pl.num_programs(1) - 1)
    def _():
        y_ref[...] = acc_ref[...]


def _moe_down(h, wd, plan, n_tiles):
    _, tile_expert, tile_row, tile_valid, _, _ = plan
    kdim = h.shape[1]
    n = wd.shape[2]
    tk = min(MOE_DOWN_K_TILE, kdim)
    return pl.pallas_call(
        _moe_down_kernel,
        grid_spec=pltpu.PrefetchScalarGridSpec(
            num_scalar_prefetch=3, grid=(n_tiles, kdim // tk),
            in_specs=[pl.BlockSpec((GROUP_TILE, tk), lambda i, k, te, tr, tv: (tr[i], k)),
                      pl.BlockSpec((None, tk, n),
                                   lambda i, k, te, tr, tv: (te[i], k * tv[i] + (kdim // tk - 1) * (1 - tv[i]), 0))],
            out_specs=pl.BlockSpec((GROUP_TILE, n), lambda i, k, te, tr, tv: (i, 0)),
            scratch_shapes=[pltpu.VMEM((GROUP_TILE, n), F32)]),
        out_shape=jax.ShapeDtypeStruct((n_tiles * GROUP_TILE, n), F32),
        compiler_params=_cparams(("arbitrary", "arbitrary")), name="moe_down",
    )(tile_expert, tile_row, tile_valid, h, wd)


def _combine_ln_kernel(dest_ref, ys_hbm, res_ref, gate_ref, g_ref, b_ref, y_ref, yb_ref, buf, sem,
                       *, tokens_per_step, n_tokens):
    base = pl.program_id(0) * tokens_per_step

    def copy(t, s):
        tok = jnp.minimum(base + t, n_tokens - 1)
        return pltpu.make_async_copy(
            ys_hbm.at[pl.ds(dest_ref[tok * TOP_K + s], 1)], buf.at[s, pl.ds(t, 1)], sem)

    def start(t, c):
        for s in range(TOP_K):
            copy(t, s).start()
        return c

    def wait(t, c):
        for s in range(TOP_K):
            copy(t, s).wait()
        return c

    lax.fori_loop(0, tokens_per_step, start, 0)
    lax.fori_loop(0, tokens_per_step, wait, 0)
    gate = gate_ref[...]
    f = gate[:, 0:1] * buf[0] + gate[:, 1:2] * buf[1]
    y = _layer_norm_rows(ALPHA * res_ref[...] + f, g_ref[...], b_ref[...])
    y_ref[...] = y
    yb_ref[...] = y.astype(BF16)


def _combine_residual_ln(ys, dest, gates, res, g, b, n_tokens):
    m, n = res.shape
    tps = COMBINE_TILE
    row = lambda i, d: (i, 0)
    vec = pl.BlockSpec((1, n), lambda i, d: (0, 0))
    return pl.pallas_call(
        functools.partial(_combine_ln_kernel, tokens_per_step=tps, n_tokens=n_tokens),
        grid_spec=pltpu.PrefetchScalarGridSpec(
            num_scalar_prefetch=1, grid=(m // tps,),
            in_specs=[pl.BlockSpec(memory_space=pl.ANY),
                      pl.BlockSpec((tps, n), row),
                      pl.BlockSpec((tps, LANES), row), vec, vec],
            out_specs=[pl.BlockSpec((tps, n), row), pl.BlockSpec((tps, n), row)],
            scratch_shapes=[pltpu.VMEM((TOP_K, tps, n), F32), pltpu.SemaphoreType.DMA(())]),
        out_shape=[jax.ShapeDtypeStruct((m, n), F32), jax.ShapeDtypeStruct((m, n), BF16)],
        compiler_params=_cparams(("arbitrary",)), name="moe_combine_ln",
    )(dest, ys, res, gates, g.reshape(1, n), b.reshape(1, n))


def _moe_block(x, xb, n_tokens, w_router, wg, wu, wd, g, b):
    del xb
    top_idx, gates = _route(x, w_router)
    n_rows = n_tokens * TOP_K
    n_tiles = -(-n_rows // GROUP_TILE) + N_EXPERTS
    plan = _routing_plan(top_idx[:n_tokens, :TOP_K], n_tiles)
    xs = _gather_rows(x, plan, n_tiles)
    h = _moe_gate_up(xs, wg, wu, plan, n_tiles)
    ys = _moe_down(h, wd, plan, n_tiles)
    return _combine_residual_ln(ys, plan[0], gates, x, g, b, n_tokens)


def _stack_heads(q):
    return jnp.concatenate([q[:, h * HEAD_DIM:(h + 1) * HEAD_DIM] for h in range(KV_GROUP)], axis=0)


def _store_heads(o_ref, o):
    blk = o.shape[0] // KV_GROUP
    for h in range(KV_GROUP):
        o_ref[:, h * HEAD_DIM:(h + 1) * HEAD_DIM] = o[h * blk:(h + 1) * blk].astype(o_ref.dtype)


def _topk_lane_mask(gate, n_cand, n_valid):
    lane = lax.broadcasted_iota(jnp.int32, gate.shape, 1)
    cnt = jnp.zeros(gate.shape, jnp.int32)
    for c in range(n_cand):
        col = gate[:, c:c + 1]
        beats = (col > gate) | ((col == gate) & (c < lane))
        cnt = cnt + jnp.where(beats, jnp.where(c < n_valid, 1, 0), 0)
    return (lane < n_valid) & (cnt < MOBA_TOPK)


def _row_chunk(c):
    return pl.ds(pl.multiple_of(c * ROW_CHUNK, ROW_CHUNK), ROW_CHUNK)


def _chunk_positions(c, width):
    q_pos = (c * ROW_CHUNK + lax.broadcasted_iota(jnp.int32, (ROW_CHUNK, width), 0)) & (ATT_BLOCK - 1)
    return q_pos, lax.broadcasted_iota(jnp.int32, (ROW_CHUNK, width), 1)


def _moba_prompt_kernel(q_ref, k_ref, v_ref, o_ref, kmean_ref, qb_ref, sel_ref, s_ref, p_ref,
                        m_ref, l_ref, a_ref, acc_ref, *, n_blk):
    qi = pl.program_id(2)
    blk = ATT_BLOCK
    n_chunks = KV_GROUP * blk // ROW_CHUNK

    @pl.when(qi == 0)
    def _():
        kmean_ref[...] = jnp.zeros_like(kmean_ref)
        for j in range(n_blk):
            kmean_ref[j:j + 1, :] = jnp.mean(k_ref[j * blk:(j + 1) * blk, :], axis=0, keepdims=True)

    qr = _stack_heads(q_ref[...])
    qb_ref[...] = qr.astype(BF16)
    sel_ref[...] = _dot_f32(qr, kmean_ref[...], nt=True)
    m_ref[...] = jnp.full(m_ref.shape, NEG_BIG, F32)
    l_ref[...] = jnp.zeros_like(l_ref)
    acc_ref[...] = jnp.zeros_like(acc_ref)

    def select(c, carry):
        r = _row_chunk(c)
        sel_ref[r, :] = _topk_lane_mask(sel_ref[r, :], n_blk - 1, qi).astype(F32)
        return carry

    lax.fori_loop(0, n_chunks, select, 0)

    def attend(j, own):
        start = pl.multiple_of(j * blk, blk)
        s_ref[...] = _dot_nt(qb_ref[...], k_ref[pl.ds(start, blk), :].astype(BF16))

        def chunk(c, carry):
            r = _row_chunk(c)
            if own:
                q_pos, k_pos = _chunk_positions(c, blk)
                keep = k_pos <= q_pos
            else:
                lane_blk = lax.broadcasted_iota(jnp.int32, (ROW_CHUNK, LANES), 1)
                keep = jnp.max(jnp.where(lane_blk == j, sel_ref[r, :], 0.0), axis=1, keepdims=True) > 0.5
            s = jnp.where(keep, s_ref[r, :] * ATTN_SCALE, NEG_BIG)
            m_old = m_ref[r, :]
            m_new = jnp.maximum(m_old, jnp.max(s, axis=1, keepdims=True))
            a = jnp.exp(m_old - m_new)
            p = jnp.exp(s - m_new)
            l_ref[r, :] = a * l_ref[r, :] + jnp.sum(p, axis=1, keepdims=True)
            m_ref[r, :] = m_new
            a_ref[r, :] = a
            p_ref[r, :] = p.astype(BF16)
            return carry

        lax.fori_loop(0, n_chunks, chunk, 0)
        acc_ref[...] = a_ref[...] * acc_ref[...] + _dot(p_ref[...], v_ref[pl.ds(start, blk), :].astype(BF16))

    attend(qi, True)

    def past(j, carry):
        attend(j, False)
        return carry

    lax.fori_loop(0, qi, past, 0)
    _store_heads(o_ref, acc_ref[...] / l_ref[...])


def _moba_prompt(q, k, v, batch, seq):
    n_q = seq // ATT_BLOCK
    gw = KV_GROUP * HEAD_DIM
    rows = KV_GROUP * ATT_BLOCK
    kv_spec = pl.BlockSpec((seq, HEAD_DIM), lambda b, g, i: (b, g))
    return pl.pallas_call(
        functools.partial(_moba_prompt_kernel, n_blk=n_q),
        grid=(batch, N_KV_HEADS, n_q),
        in_specs=[pl.BlockSpec((ATT_BLOCK, gw), lambda b, g, i: (b * n_q + i, g)), kv_spec, kv_spec],
        out_specs=pl.BlockSpec((ATT_BLOCK, gw), lambda b, g, i: (b * n_q + i, g)),
        out_shape=jax.ShapeDtypeStruct((batch * seq, D_MODEL), BF16),
        scratch_shapes=[pltpu.VMEM((LANES, HEAD_DIM), F32),
                        pltpu.VMEM((rows, HEAD_DIM), BF16),
                        pltpu.VMEM((rows, LANES), F32),
                        pltpu.VMEM((rows, ATT_BLOCK), F32),
                        pltpu.VMEM((rows, ATT_BLOCK), BF16),
                        pltpu.VMEM((rows, 1), F32), pltpu.VMEM((rows, 1), F32), pltpu.VMEM((rows, 1), F32),
                        pltpu.VMEM((rows, HEAD_DIM), F32)],
        compiler_params=_cparams(("parallel", "parallel", "arbitrary")), name="moba_prompt")(q, k, v)


def _log_sigmoid_pair(z):
    t = jnp.log1p(jnp.exp(-jnp.abs(z)))
    return jnp.minimum(z, 0.0) - t, -jnp.maximum(z, 0.0) - t


def _suffix_sums(x, upper):
    hi = x.astype(BF16)
    lo = (x - hi.astype(F32)).astype(BF16)
    return _dot(hi, upper) + _dot(lo, upper)


def _strict_upper(n):
    r = lax.broadcasted_iota(jnp.int32, (n, n), 0)
    c = lax.broadcasted_iota(jnp.int32, (n, n), 1)
    return jnp.where(r > c, 1.0, 0.0).astype(BF16)


def _still_alive(carry_ref):
    return (jnp.max(carry_ref[...]) >= EXP_ZERO_BELOW).astype(jnp.int32)


def _sb_prompt_kernel(q_ref, k_ref, v_ref, o_ref, qb_ref, z_ref, lat_ref, hi_ref, lo_ref, w_ref,
                      carry_ref, rs_ref, acc_ref):
    qi = pl.program_id(2)
    blk = ATT_BLOCK
    n_chunks = KV_GROUP * blk // ROW_CHUNK
    qb_ref[...] = _stack_heads(q_ref[...]).astype(BF16)
    carry_ref[...] = jnp.zeros_like(carry_ref)
    acc_ref[...] = jnp.zeros_like(acc_ref)
    upper = _strict_upper(blk)

    def attend(j, diagonal):
        start = pl.multiple_of(j * blk, blk)
        z_ref[...] = _dot_nt(qb_ref[...], k_ref[pl.ds(start, blk), :].astype(BF16))

        def causal(c):
            q_pos, k_pos = _chunk_positions(c, blk)
            return k_pos < q_pos

        def keep_terms(c, carry):
            r = _row_chunk(c)
            log_beta, log_keep = _log_sigmoid_pair(z_ref[r, :] * ATTN_SCALE)
            if diagonal:
                log_keep = jnp.where(causal(c), log_keep, 0.0)
            hi = log_keep.astype(BF16)
            hi_ref[r, :] = hi
            lo_ref[r, :] = (log_keep - hi.astype(F32)).astype(BF16)
            z_ref[r, :] = log_beta + carry_ref[r, :]
            rs_ref[r, :] = jnp.sum(log_keep, axis=1, keepdims=True)
            return carry

        lax.fori_loop(0, n_chunks, keep_terms, 0)
        lat_ref[...] = _dot(hi_ref[...], upper) + _dot(lo_ref[...], upper)

        def weights(c, carry):
            r = _row_chunk(c)
            w = jnp.exp(z_ref[r, :] + lat_ref[r, :])
            if diagonal:
                w = jnp.where(causal(c), w, 0.0)
            w_ref[r, :] = w.astype(BF16)
            return carry

        lax.fori_loop(0, n_chunks, weights, 0)
        acc_ref[...] += _dot(w_ref[...], v_ref[pl.ds(start, blk), :].astype(BF16))
        carry_ref[...] += rs_ref[...]

    attend(qi, True)

    def more(state):
        t, alive = state
        return (t < qi) & (alive == 1)

    def older(state):
        t, _ = state
        attend(qi - 1 - t, False)
        return t + 1, _still_alive(carry_ref)

    lax.while_loop(more, older, (jnp.int32(0), _still_alive(carry_ref)))
    _store_heads(o_ref, acc_ref[...])


def _sb_prompt(q, k, v, batch, seq):
    n_q = seq // ATT_BLOCK
    gw = KV_GROUP * HEAD_DIM
    rows = KV_GROUP * ATT_BLOCK
    kv_spec = pl.BlockSpec((seq, HEAD_DIM), lambda b, g, i: (b, g))
    return pl.pallas_call(
        _sb_prompt_kernel, grid=(batch, N_KV_HEADS, n_q),
        in_specs=[pl.BlockSpec((ATT_BLOCK, gw), lambda b, g, i: (b * n_q + i, g)), kv_spec, kv_spec],
        out_specs=pl.BlockSpec((ATT_BLOCK, gw), lambda b, g, i: (b * n_q + i, g)),
        out_shape=jax.ShapeDtypeStruct((batch * seq, D_MODEL), BF16),
        scratch_shapes=[pltpu.VMEM((rows, HEAD_DIM), BF16),
                        pltpu.VMEM((rows, ATT_BLOCK), F32),
                        pltpu.VMEM((rows, ATT_BLOCK), F32),
                        pltpu.VMEM((rows, ATT_BLOCK), BF16), pltpu.VMEM((rows, ATT_BLOCK), BF16),
                        pltpu.VMEM((rows, ATT_BLOCK), BF16),
                        pltpu.VMEM((rows, 1), F32), pltpu.VMEM((rows, 1), F32),
                        pltpu.VMEM((rows, HEAD_DIM), F32)],
        compiler_params=_cparams(("parallel", "parallel", "arbitrary")), name="sb_prompt")(q, k, v)


def _group_rows(q, dec_batch, dec_seq):
    q = q.reshape(dec_batch, dec_seq, N_KV_HEADS, KV_GROUP, HEAD_DIM)
    return q.transpose(0, 2, 1, 3, 4).reshape(dec_batch, N_KV_HEADS, dec_seq * KV_GROUP, HEAD_DIM)


def _ungroup_rows(o, dec_batch, dec_seq):
    o = o.reshape(dec_batch, N_KV_HEADS, dec_seq, KV_GROUP, HEAD_DIM)
    return o.transpose(0, 2, 1, 3, 4).reshape(dec_batch * dec_seq, D_MODEL)


def _pad_new(x, dec_batch, dec_seq):
    x = x.reshape(dec_batch, dec_seq, KV_WIDTH)
    return jnp.pad(x, ((0, 0), (0, PAGE_SIZE - dec_seq), (0, 0)))


def _page_pair(ref0, ref1, g):
    return jnp.concatenate([ref0[:, g, :], ref1[:, g, :]], axis=0).astype(BF16)


def _moba_sample_kernel(pt_ref, q_ref, kn_ref, vn_ref, *refs, n_blk, dec_seq):
    del pt_ref
    pages = PAGES_PER_STEP
    k_refs, v_refs = refs[:pages], refs[pages:2 * pages]
    o_ref, s_ref, p_ref, kmean_ref, acc_ref, l_ref = refs[2 * pages:]
    step = pl.program_id(1)
    rg = dec_seq * KV_GROUP
    bps = pages // 2
    k_steps = n_blk // bps

    @pl.when(step == 0)
    def _():
        kmean_ref[...] = jnp.zeros_like(kmean_ref)

    @pl.when(step < k_steps)
    def _():
        for g in range(N_KV_HEADS):
            qb = q_ref[g].astype(BF16)
            for bl in range(bps):
                j = step * bps + bl
                kf = jnp.concatenate([k_refs[2 * bl][:, g, :], k_refs[2 * bl + 1][:, g, :]], axis=0)
                kmean_ref[g, pl.ds(j, 1), :] = jnp.mean(kf, axis=0, keepdims=True)
                s_ref[j, g * rg:(g + 1) * rg, :] = _dot_nt(qb, kf.astype(BF16)) * ATTN_SCALE

    @pl.when(step == k_steps)
    def _():
        sel_g, s_new_g = [], []
        tok = lax.broadcasted_iota(jnp.int32, (rg, PAGE_SIZE), 0) // KV_GROUP
        key = lax.broadcasted_iota(jnp.int32, (rg, PAGE_SIZE), 1)
        for g in range(N_KV_HEADS):
            gate = _dot_f32(q_ref[g], kmean_ref[g], nt=True)
            sel_g.append(_topk_lane_mask(gate, n_blk, n_blk).astype(F32))
            kn = kn_ref[:, g * HEAD_DIM:(g + 1) * HEAD_DIM].astype(BF16)
            s_new = _dot_nt(q_ref[g].astype(BF16), kn) * ATTN_SCALE
            s_new_g.append(jnp.where(key <= tok, s_new, NEG_BIG))
        sel = jnp.concatenate(sel_g, axis=0)
        s_new = jnp.concatenate(s_new_g, axis=0)
        m = jnp.max(s_new, axis=1, keepdims=True)
        for j in range(n_blk):
            picked = sel[:, j:j + 1] > 0.5
            m = jnp.maximum(m, jnp.max(jnp.where(picked, s_ref[j], NEG_BIG), axis=1, keepdims=True))
        p_new = jnp.exp(s_new - m)
        l = jnp.sum(p_new, axis=1, keepdims=True)
        for j in range(n_blk):
            picked = sel[:, j:j + 1] > 0.5
            p = jnp.where(picked, jnp.exp(s_ref[j] - m), 0.0)
            l = l + jnp.sum(p, axis=1, keepdims=True)
            p_ref[j] = p.astype(BF16)
        l_ref[...] = l
        for g in range(N_KV_HEADS):
            vn = vn_ref[:, g * HEAD_DIM:(g + 1) * HEAD_DIM].astype(BF16)
            acc_ref[g * rg:(g + 1) * rg, :] = _dot(p_new[g * rg:(g + 1) * rg].astype(BF16), vn)

    @pl.when(step >= k_steps)
    def _():
        for g in range(N_KV_HEADS):
            part = None
            for bl in range(bps):
                j = (step - k_steps) * bps + bl
                t = _dot(p_ref[j, g * rg:(g + 1) * rg, :], _page_pair(v_refs[2 * bl], v_refs[2 * bl + 1], g))
                part = t if part is None else part + t
            acc_ref[g * rg:(g + 1) * rg, :] += part

    @pl.when(step == 2 * k_steps - 1)
    def _():
        out = acc_ref[...] / l_ref[...]
        for g in range(N_KV_HEADS):
            o_ref[g] = out[g * rg:(g + 1) * rg]


def _moba_sample(qg, k_new, v_new, cache_k, cache_v, page_table, dec_seq):
    dec_batch, n_pages = page_table.shape
    n_blk = n_pages // 2
    rg = dec_seq * KV_GROUP
    rows = N_KV_HEADS * rg
    pages = PAGES_PER_STEP
    assert n_pages % pages == 0
    k_steps = n_pages // pages
    page = (None, PAGE_SIZE, N_KV_HEADS, HEAD_DIM)

    def k_map(off):
        return lambda b, s, pt: (pt[b, pages * jnp.minimum(s, k_steps - 1) + off], 0, 0, 0)

    def v_map(off):
        return lambda b, s, pt: (pt[b, pages * jnp.maximum(s - k_steps, 0) + off], 0, 0, 0)

    per_b4 = pl.BlockSpec((None, N_KV_HEADS, rg, HEAD_DIM), lambda b, s, pt: (b, 0, 0, 0))
    per_b3 = pl.BlockSpec((None, PAGE_SIZE, KV_WIDTH), lambda b, s, pt: (b, 0, 0))
    return pl.pallas_call(
        functools.partial(_moba_sample_kernel, n_blk=n_blk, dec_seq=dec_seq),
        grid_spec=pltpu.PrefetchScalarGridSpec(
            num_scalar_prefetch=1, grid=(dec_batch, 2 * k_steps),
            in_specs=[per_b4, per_b3, per_b3]
            + [pl.BlockSpec(page, k_map(i)) for i in range(pages)]
            + [pl.BlockSpec(page, v_map(i)) for i in range(pages)],
            out_specs=per_b4,
            scratch_shapes=[pltpu.VMEM((n_blk, rows, 2 * PAGE_SIZE), F32),
                            pltpu.VMEM((n_blk, rows, 2 * PAGE_SIZE), BF16),
                            pltpu.VMEM((N_KV_HEADS, LANES, HEAD_DIM), F32),
                            pltpu.VMEM((rows, HEAD_DIM), F32),
                            pltpu.VMEM((rows, 1), F32)]),
        out_shape=jax.ShapeDtypeStruct((dec_batch, N_KV_HEADS, rg, HEAD_DIM), F32),
        compiler_params=_cparams(("parallel", "arbitrary")), name="moba_sample",
    )(page_table, qg, k_new, v_new, *([cache_k] * pages), *([cache_v] * pages))


def _sb_sample_kernel(pt_ref, q_ref, kn_ref, vn_ref, *refs, dec_seq):
    del pt_ref
    pages = PAGES_PER_STEP
    k_refs, v_refs = refs[:pages], refs[pages:2 * pages]
    o_ref, acc_ref, carry_ref, alive_ref = refs[2 * pages:]
    step = pl.program_id(1)
    rg = dec_seq * KV_GROUP
    blk = 2 * PAGE_SIZE

    def accumulate(g, kb, vb, upper, causal):
        rows = slice(g * rg, (g + 1) * rg)
        z = _dot_nt(q_ref[g].astype(BF16), kb) * ATTN_SCALE
        log_beta, log_keep = _log_sigmoid_pair(z)
        if causal is not None:
            log_keep = jnp.where(causal, log_keep, 0.0)
        w = jnp.exp(log_beta + _suffix_sums(log_keep, upper) + carry_ref[rows, :])
        if causal is not None:
            w = jnp.where(causal, w, 0.0)
        acc_ref[rows, :] += _dot(w.astype(BF16), vb)
        carry_ref[rows, :] += jnp.sum(log_keep, axis=1, keepdims=True)

    @pl.when(step == 0)
    def _():
        acc_ref[...] = jnp.zeros_like(acc_ref)
        carry_ref[...] = jnp.zeros_like(carry_ref)
        tok = lax.broadcasted_iota(jnp.int32, (rg, PAGE_SIZE), 0) // KV_GROUP
        key = lax.broadcasted_iota(jnp.int32, (rg, PAGE_SIZE), 1)
        upper = _strict_upper(PAGE_SIZE)
        for g in range(N_KV_HEADS):
            cols = slice(g * HEAD_DIM, (g + 1) * HEAD_DIM)
            accumulate(g, kn_ref[:, cols].astype(BF16), vn_ref[:, cols].astype(BF16), upper, key < tok)
        alive_ref[0] = _still_alive(carry_ref)

    @pl.when(alive_ref[0] == 1)
    def _():
        upper = _strict_upper(blk)
        for bl in reversed(range(pages // 2)):
            for g in range(N_KV_HEADS):
                accumulate(g, _page_pair(k_refs[2 * bl], k_refs[2 * bl + 1], g),
                           _page_pair(v_refs[2 * bl], v_refs[2 * bl + 1], g), upper, None)
        alive_ref[0] = _still_alive(carry_ref)

    @pl.when(step == pl.num_programs(1) - 1)
    def _():
        for g in range(N_KV_HEADS):
            o_ref[g] = acc_ref[g * rg:(g + 1) * rg, :]


def _sb_sample(qg, k_new, v_new, cache_k, cache_v, page_table, dec_seq):
    dec_batch, n_pages = page_table.shape
    rg = dec_seq * KV_GROUP
    rows = N_KV_HEADS * rg
    pages = PAGES_PER_STEP
    assert n_pages % pages == 0
    page = (None, PAGE_SIZE, N_KV_HEADS, HEAD_DIM)

    def page_map(off):
        return lambda b, s, pt: (pt[b, n_pages - pages * (s + 1) + off], 0, 0, 0)

    per_b4 = pl.BlockSpec((None, N_KV_HEADS, rg, HEAD_DIM), lambda b, s, pt: (b, 0, 0, 0))
    per_b3 = pl.BlockSpec((None, PAGE_SIZE, KV_WIDTH), lambda b, s, pt: (b, 0, 0))
    page_specs = [pl.BlockSpec(page, page_map(i)) for i in range(pages)]
    return pl.pallas_call(
        functools.partial(_sb_sample_kernel, dec_seq=dec_seq),
        grid_spec=pltpu.PrefetchScalarGridSpec(
            num_scalar_prefetch=1, grid=(dec_batch, n_pages // pages),
            in_specs=[per_b4, per_b3, per_b3] + page_specs + page_specs,
            out_specs=per_b4,
            scratch_shapes=[pltpu.VMEM((rows, HEAD_DIM), F32), pltpu.VMEM((rows, 1), F32),
                            pltpu.SMEM((1,), jnp.int32)]),
        out_shape=jax.ShapeDtypeStruct((dec_batch, N_KV_HEADS, rg, HEAD_DIM), F32),
        compiler_params=_cparams(("parallel", "arbitrary")), name="sb_sample",
    )(page_table, qg, k_new, v_new, *([cache_k] * pages), *([cache_v] * pages))


def _pool_kernel(x_ref, w_ref, scale_ref, o_ref, *, seq, chunk, counts):
    g = pl.program_id(1)
    halo = POOL_STATE + 1
    w = w_ref[...].astype(BF16)
    for c0 in range(0, seq, chunk):
        cur = x_ref[halo + c0:halo + c0 + chunk, :]
        tot, out = cur, None
        for d in range(1, max(POOL_WINDOWS)):
            tot = tot + x_ref[halo + c0 - d:halo + c0 - d + chunk, :]
            if d + 1 in POOL_WINDOWS:
                gi = POOL_WINDOWS.index(d + 1)
                mixed = tot / counts(gi, c0, chunk) - cur
                out = mixed if out is None else jnp.where(g == gi, mixed, out)
        o_ref[c0:c0 + chunk, :] = _dot(out.astype(BF16), w) * scale_ref[...]


def _pool_mixer(x_hist, w_pool, scale, seq, first_pos):
    batch = x_hist.shape[0]
    halo = POOL_STATE + 1
    chunk = min(seq, 512)

    def counts(gi, c0, rows):
        win = POOL_WINDOWS[gi]
        pos = first_pos + c0 + lax.broadcasted_iota(jnp.int32, (rows, 1), 0)
        return jnp.minimum(win, pos + 1).astype(F32)

    return pl.pallas_call(
        functools.partial(_pool_kernel, seq=seq, chunk=chunk, counts=counts),
        grid=(batch, len(POOL_WINDOWS)),
        in_specs=[pl.BlockSpec((None, halo + seq, POOL_GROUP), lambda b, g: (b, 0, g)),
                  pl.BlockSpec((None, POOL_GROUP, POOL_GROUP), lambda b, g: (g, 0, 0)),
                  pl.BlockSpec((1, POOL_GROUP), lambda b, g: (0, g))],
        out_specs=pl.BlockSpec((None, seq, POOL_GROUP), lambda b, g: (b, 0, g)),
        out_shape=jax.ShapeDtypeStruct((batch, seq, D_MODEL), F32),
        compiler_params=_cparams(("parallel", "parallel")), name="pool_mixer",
    )(x_hist, w_pool, scale.reshape(1, D_MODEL))


def _rope_tables(pos):
    half = ROT_DIM // 2
    inv = ROPE_THETA ** (-jnp.arange(0, ROT_DIM, 2, dtype=F32) / ROT_DIM)
    ang = pos.astype(F32)[:, None] * inv[None, :]
    cos, sin = jnp.cos(ang), jnp.sin(ang)
    n = pos.shape[0]
    ones = jnp.ones((n, HEAD_DIM - ROT_DIM), F32)
    zeros = jnp.zeros((n, HEAD_DIM - half), F32)
    a = jnp.concatenate([cos, cos, ones], axis=1)
    b = jnp.concatenate([-sin, zeros], axis=1)
    c = jnp.concatenate([jnp.zeros((n, half), F32), sin, jnp.zeros((n, HEAD_DIM - ROT_DIM), F32)], axis=1)
    return a, b, c


def kernel(x_prompt, x_sample, cache_k_l0, cache_v_l0, state_pool_l1, cache_k_l2, cache_v_l2, cache_k_l3, cache_v_l3, page_table, l0_wq, l0_wk, l0_wv, l0_wo, l0_ln1_g, l0_ln1_b, l0_ffn_wg, l0_ffn_wu, l0_ffn_wd, l0_ln2_g, l0_ln2_b, l1_pool_w, l1_pool_scale, l1_ln1_g, l1_ln1_b, l1_router, l1_moe_wg, l1_moe_wu, l1_moe_wd, l1_ln2_g, l1_ln2_b, l2_wq, l2_wk, l2_wv, l2_wo, l2_ln1_g, l2_ln1_b, l2_ffn_wg, l2_ffn_wu, l2_ffn_wd, l2_ln2_g, l2_ln2_b, l3_wq, l3_wk, l3_wv, l3_wo, l3_ln1_g, l3_ln1_b, l3_router, l3_moe_wg, l3_moe_wu, l3_moe_wd, l3_ln2_g, l3_ln2_b):
    batch, seq, _ = x_prompt.shape
    dec_batch, dec_seq, _ = x_sample.shape
    n_prompt = batch * seq
    n_sample = dec_batch * dec_seq
    n_tokens = n_prompt + n_sample
    m_pad = -(-n_tokens // ROW_TILE) * ROW_TILE
    assert m_pad % LN_ROW_TILE == 0 and seq % ATT_BLOCK == 0 and PAST_LEN % MOBA_BLOCK == 0
    assert PAST_LEN // MOBA_BLOCK == (PAST_LEN + dec_seq - 1) // MOBA_BLOCK and dec_seq <= PAGE_SIZE

    x = jnp.concatenate([x_prompt.reshape(n_prompt, D_MODEL), x_sample.reshape(n_sample, D_MODEL),
                         jnp.zeros((m_pad - n_tokens, D_MODEL), F32)], axis=0)
    xb = x.astype(BF16)
    pos = jnp.concatenate([jnp.tile(jnp.arange(seq, dtype=jnp.int32), batch),
                           jnp.tile(PAST_LEN + jnp.arange(dec_seq, dtype=jnp.int32), dec_batch),
                           jnp.zeros((m_pad - n_tokens,), jnp.int32)])
    rope = _rope_tables(pos)

    def split(y, width):
        p = y[:n_prompt].reshape(batch, seq, width)
        s = y[n_prompt:n_tokens].reshape(dec_batch, dec_seq, width)
        return p, s

    def attention_layer(kind, x, xb, w, caches):
        wq, wk, wv, wo = w
        q = _project(xb, wq, rope if kind == 0 else None)
        k = _project(xb, wk, rope if kind == 0 else None)
        v = _project(xb, wv)
        prompt_fn = _moba_prompt if kind == 0 else _sb_prompt
        sample_fn = _moba_sample if kind == 0 else _sb_sample
        o_p = prompt_fn(q, k, v, batch, seq)
        qg = _group_rows(q[n_prompt:n_tokens], dec_batch, dec_seq)
        o_s = sample_fn(qg, _pad_new(k[n_prompt:n_tokens], dec_batch, dec_seq),
                        _pad_new(v[n_prompt:n_tokens], dec_batch, dec_seq), caches[0], caches[1], page_table, dec_seq)
        o_s = _ungroup_rows(o_s, dec_batch, dec_seq).astype(BF16)
        o = jnp.concatenate([o_p, o_s, jnp.zeros((m_pad - n_tokens, D_MODEL), BF16)], axis=0)
        kp, ks = split(k, KV_WIDTH)
        vp, vs = split(v, KV_WIDTH)
        shape4 = lambda t: t.reshape(t.shape[0], t.shape[1], N_KV_HEADS, HEAD_DIM)
        return o, wo, (shape4(kp), shape4(vp), shape4(ks), shape4(vs))

    def pool_layer(x):
        xp, xs = split(x, D_MODEL)
        hist_p = jnp.concatenate([jnp.zeros((batch, POOL_STATE + 1, D_MODEL), F32), xp], axis=1)
        seq_s = -(-dec_seq // SUBLANES) * SUBLANES
        hist_s = jnp.concatenate([jnp.zeros((dec_batch, 1, D_MODEL), F32), state_pool_l1, xs,
                                  jnp.zeros((dec_batch, seq_s - dec_seq, D_MODEL), F32)], axis=1)
        h_p = _pool_mixer(hist_p, l1_pool_w, l1_pool_scale, seq, 0)
        h_s = _pool_mixer(hist_s, l1_pool_w, l1_pool_scale, seq_s, PAST_LEN)[:, :dec_seq]
        h = jnp.concatenate([h_p.reshape(n_prompt, D_MODEL), h_s.reshape(n_sample, D_MODEL),
                             jnp.zeros((m_pad - n_tokens, D_MODEL), F32)], axis=0)
        state_p = hist_p[:, -POOL_STATE:]
        state_s = jnp.concatenate([state_pool_l1, xs], axis=1)[:, -POOL_STATE:]
        return h, (state_p, state_s)

    def dense_ffn(x, xb, w, g, b):
        wg, wu, wd = w
        return _down_residual_ln(_gate_up(xb, wg, wu), wd, x, g, b)

    o, wo, kv0 = attention_layer(0, x, xb, (l0_wq, l0_wk, l0_wv, l0_wo), (cache_k_l0, cache_v_l0))
    x, xb = _down_residual_ln(o, wo, x, l0_ln1_g, l0_ln1_b)
    x, xb = dense_ffn(x, xb, (l0_ffn_wg, l0_ffn_wu, l0_ffn_wd), l0_ln2_g, l0_ln2_b)
    h, pool_state = pool_layer(x)
    x, xb = _residual_ln(x, h, l1_ln1_g, l1_ln1_b)
    x, xb = _moe_block(x, xb, n_tokens, l1_router, l1_moe_wg, l1_moe_wu, l1_moe_wd, l1_ln2_g, l1_ln2_b)
    o, wo, kv2 = attention_layer(2, x, xb, (l2_wq, l2_wk, l2_wv, l2_wo), (cache_k_l2, cache_v_l2))
    x, xb = _down_residual_ln(o, wo, x, l2_ln1_g, l2_ln1_b)
    x, xb = dense_ffn(x, xb, (l2_ffn_wg, l2_ffn_wu, l2_ffn_wd), l2_ln2_g, l2_ln2_b)
    o, wo, kv3 = attention_layer(0, x, xb, (l3_wq, l3_wk, l3_wv, l3_wo), (cache_k_l3, cache_v_l3))
    x, xb = _down_residual_ln(o, wo, x, l3_ln1_g, l3_ln1_b)
    x, xb = _moe_block(x, xb, n_tokens, l3_router, l3_moe_wg, l3_moe_wu, l3_moe_wd, l3_ln2_g, l3_ln2_b)

    y_prompt, y_sample = split(x, D_MODEL)
    return (y_prompt, y_sample, kv0[0], kv0[1], kv0[2], kv0[3], pool_state[0], pool_state[1],
            kv2[0], kv2[1], kv2[2], kv2[3], kv3[0], kv3[1], kv3[2], kv3[3])
```

```python
import functools
import math

import jax
import jax.numpy as jnp
from jax import lax
from jax.experimental import pallas as pl
from jax.experimental.pallas import tpu as pltpu

F32 = jnp.float32
BF16 = jnp.bfloat16

D_MODEL = 2048
DEPTH = 4
PAST_LEN = 16384
PAGE_SIZE = 128
N_HEADS = 16
N_KV_HEADS = 4
HEAD_DIM = D_MODEL // N_HEADS
KV_GROUP = N_HEADS // N_KV_HEADS
KV_WIDTH = N_KV_HEADS * HEAD_DIM
ROT_DIM = HEAD_DIM // 4
ROPE_THETA = 500000.0
MOBA_BLOCK = 256
MOBA_TOPK = 3
POOL_WINDOWS = (2, 4, 8, 16)
POOL_GROUP = D_MODEL // len(POOL_WINDOWS)
POOL_STATE = max(POOL_WINDOWS) - 1
N_EXPERTS = 8
TOP_K = 2
ALPHA = (2 * DEPTH) ** 0.25
LN_EPS = 1e-5
ATTN_SCALE = HEAD_DIM ** -0.5

LANES = 128
SUBLANES = 8
VMEM_LIMIT_BYTES = 56 * 1024 * 1024

ROW_TILE = 1056
LN_ROW_TILE = 528
FF_TILE = 512
GROUP_TILE = 512
MOE_DOWN_K_TILE = 1024
COMBINE_TILE = 264
DOWN_ROW_TILE = 1056
DOWN_K_TILE = 256
LN_CHUNK = 96
ROW_CHUNK = 64
CHUNK_UNROLL = 16
PAGES_PER_STEP = 8
ATT_BLOCK = 256
NEG_BIG = -1e30
EXP_ZERO_BELOW = -104.0


def _cparams(sem):
    return pltpu.CompilerParams(dimension_semantics=sem, vmem_limit_bytes=VMEM_LIMIT_BYTES)


def _dot(a, b):
    return jnp.dot(a, b, preferred_element_type=F32)


def _dot_nt(a, b):
    return lax.dot_general(a, b, (((1,), (1,)), ((), ())), preferred_element_type=F32)


def _split_bf16(x, n):
    parts, r = [], x
    for _ in range(n):
        p = r.astype(BF16)
        parts.append(p)
        r = r - p.astype(F32)
    return parts


def _dot_f32(a, b, nt=False):
    a3, b3 = _split_bf16(a, 3), _split_bf16(b, 3)
    dot = _dot_nt if nt else _dot
    out = None
    for i, j in ((0, 2), (1, 1), (2, 0), (0, 1), (1, 0), (0, 0)):
        t = dot(a3[i], b3[j])
        out = t if out is None else out + t
    return out


def _layer_norm_rows(y, g, b):
    mu = jnp.mean(y, axis=-1, keepdims=True)
    d = y - mu
    var = jnp.mean(d * d, axis=-1, keepdims=True)
    return d * lax.rsqrt(var + LN_EPS) * g + b


def _proj_kernel(x_ref, w_ref, o_ref):
    o_ref[...] = _dot(x_ref[...].astype(BF16), w_ref[...].astype(BF16))


def _proj_rope_kernel(x_ref, w_ref, ca_ref, sb_ref, sc_ref, o_ref):
    y = _dot(x_ref[...].astype(BF16), w_ref[...].astype(BF16))
    half = ROT_DIM // 2
    ca, sb, sc = ca_ref[...], sb_ref[...], sc_ref[...]
    for h in range(y.shape[1] // HEAD_DIM):
        yh = y[:, h * HEAD_DIM:(h + 1) * HEAD_DIM]
        up = pltpu.roll(yh, HEAD_DIM - half, axis=1)
        down = pltpu.roll(yh, half, axis=1)
        o_ref[:, h * HEAD_DIM:(h + 1) * HEAD_DIM] = yh * ca + up * sb + down * sc


def _project(x, w, rope=None, col_tile=512):
    m, kdim = x.shape
    n = w.shape[1]
    tn = min(col_tile, n)
    grid = (m // ROW_TILE, n // tn)
    x_spec = pl.BlockSpec((ROW_TILE, kdim), lambda i, j: (i, 0))
    w_spec = pl.BlockSpec((kdim, tn), lambda i, j: (0, j))
    o_spec = pl.BlockSpec((ROW_TILE, tn), lambda i, j: (i, j))
    if rope is None:
        return pl.pallas_call(
            _proj_kernel, grid=grid, in_specs=[x_spec, w_spec], out_specs=o_spec,
            out_shape=jax.ShapeDtypeStruct((m, n), F32),
            compiler_params=_cparams(("parallel", "arbitrary")), name="project")(x, w)
    t_spec = pl.BlockSpec((ROW_TILE, HEAD_DIM), lambda i, j: (i, 0))
    return pl.pallas_call(
        _proj_rope_kernel, grid=grid, in_specs=[x_spec, w_spec, t_spec, t_spec, t_spec],
        out_specs=o_spec, out_shape=jax.ShapeDtypeStruct((m, n), F32),
        compiler_params=_cparams(("parallel", "arbitrary")), name="project_rope")(x, w, *rope)


def _gateup_kernel(x_ref, wg_ref, wu_ref, h_ref):
    x = x_ref[...].astype(BF16)
    a = _dot(x, wg_ref[...].astype(BF16))
    u = _dot(x, wu_ref[...].astype(BF16))
    h_ref[...] = (a * (1.0 / (1.0 + jnp.exp(-a))) * u).astype(h_ref.dtype)


def _gate_up(x, wg, wu):
    m, kdim = x.shape
    n = wg.shape[1]
    grid = (m // ROW_TILE, n // FF_TILE)
    return pl.pallas_call(
        _gateup_kernel, grid=grid,
        in_specs=[pl.BlockSpec((ROW_TILE, kdim), lambda i, j: (i, 0)),
                  pl.BlockSpec((kdim, FF_TILE), lambda i, j: (0, j)),
                  pl.BlockSpec((kdim, FF_TILE), lambda i, j: (0, j))],
        out_specs=pl.BlockSpec((ROW_TILE, FF_TILE), lambda i, j: (i, j)),
        out_shape=jax.ShapeDtypeStruct((m, n), BF16),
        compiler_params=_cparams(("parallel", "arbitrary")), name="gate_up")(x, wg, wu)


def _down_ln_kernel(h_ref, w_ref, res_ref, g_ref, b_ref, y_ref, yb_ref, acc_ref):
    k = pl.program_id(1)

    @pl.when(k == 0)
    def _():
        acc_ref[...] = jnp.zeros_like(acc_ref)

    acc_ref[...] += _dot(h_ref[...].astype(BF16), w_ref[...].astype(BF16))

    @pl.when(k == pl.num_programs(1) - 1)
    def _():
        rows = acc_ref.shape[0]
        chunk = math.gcd(rows, LN_CHUNK)

        def body(c, carry):
            r = pl.ds(pl.multiple_of(c * chunk, 2 * SUBLANES), chunk)
            y = _layer_norm_rows(ALPHA * res_ref[r, :] + acc_ref[r, :], g_ref[...], b_ref[...])
            y_ref[r, :] = y
            yb_ref[r, :] = y.astype(BF16)
            return carry

        lax.fori_loop(0, rows // chunk, body, 0)


def _down_residual_ln(h, w, res, g, b):
    m, kdim = h.shape
    n = w.shape[1]
    tm = DOWN_ROW_TILE
    tk = min(DOWN_K_TILE, kdim)
    grid = (m // tm, kdim // tk)
    row = lambda i, k: (i, 0)
    return pl.pallas_call(
        _down_ln_kernel, grid=grid,
        in_specs=[pl.BlockSpec((tm, tk), lambda i, k: (i, k)),
                  pl.BlockSpec((tk, n), lambda i, k: (k, 0)),
                  pl.BlockSpec((tm, n), row, pipeline_mode=pl.Buffered(1)),
                  pl.BlockSpec((1, n), lambda i, k: (0, 0)),
                  pl.BlockSpec((1, n), lambda i, k: (0, 0))],
        out_specs=[pl.BlockSpec((tm, n), row), pl.BlockSpec((tm, n), row)],
        out_shape=[jax.ShapeDtypeStruct((m, n), F32), jax.ShapeDtypeStruct((m, n), BF16)],
        scratch_shapes=[pltpu.VMEM((tm, n), F32)],
        compiler_params=_cparams(("parallel", "arbitrary")), name="down_residual_ln",
    )(h, w, res, g.reshape(1, n), b.reshape(1, n))


def _residual_ln_kernel(res_ref, h_ref, g_ref, b_ref, y_ref, yb_ref):
    y = _layer_norm_rows(ALPHA * res_ref[...] + h_ref[...], g_ref[...], b_ref[...])
    y_ref[...] = y
    yb_ref[...] = y.astype(BF16)


def _residual_ln(res, h, g, b):
    m, n = res.shape
    row = pl.BlockSpec((LN_ROW_TILE, n), lambda i: (i, 0))
    vec = pl.BlockSpec((1, n), lambda i: (0, 0))
    return pl.pallas_call(
        _residual_ln_kernel, grid=(m // LN_ROW_TILE,), in_specs=[row, row, vec, vec],
        out_specs=[row, row],
        out_shape=[jax.ShapeDtypeStruct((m, n), F32), jax.ShapeDtypeStruct((m, n), BF16)],
        compiler_params=_cparams(("parallel",)), name="residual_ln",
    )(res, h, g.reshape(1, n), b.reshape(1, n))


def _router_kernel(x_ref, w_ref, idx_ref, gate_ref):
    logits = _dot_f32(x_ref[...], w_ref[...])
    lane = lax.broadcasted_iota(jnp.int32, logits.shape, 1)
    logits = jnp.where(lane < N_EXPERTS, logits, -jnp.inf)
    v1 = jnp.max(logits, axis=1, keepdims=True)
    i1 = jnp.min(jnp.where(logits == v1, lane, LANES), axis=1, keepdims=True)
    rest = jnp.where(lane == i1, -jnp.inf, logits)
    v2 = jnp.max(rest, axis=1, keepdims=True)
    i2 = jnp.min(jnp.where(rest == v2, lane, LANES), axis=1, keepdims=True)
    e = jnp.exp(v2 - v1)
    g1 = 1.0 / (1.0 + e)
    g2 = e / (1.0 + e)
    idx_ref[...] = jnp.where(lane == 0, i1, jnp.where(lane == 1, i2, 0))
    gate_ref[...] = jnp.where(lane == 0, g1, jnp.where(lane == 1, g2, 0.0))


def _route(x, w_router):
    m, kdim = x.shape
    w_pad = jnp.zeros((kdim, LANES), F32).at[:, :N_EXPERTS].set(w_router)
    out = pl.BlockSpec((LN_ROW_TILE, LANES), lambda i: (i, 0))
    return pl.pallas_call(
        _router_kernel, grid=(m // LN_ROW_TILE,),
        in_specs=[pl.BlockSpec((LN_ROW_TILE, kdim), lambda i: (i, 0)),
                  pl.BlockSpec((kdim, LANES), lambda i: (0, 0))],
        out_specs=[out, out],
        out_shape=[jax.ShapeDtypeStruct((m, LANES), jnp.int32), jax.ShapeDtypeStruct((m, LANES), F32)],
        compiler_params=_cparams(("parallel",)), name="router")(x, w_pad)


def _routing_plan(top_idx, n_tiles):
    e_flat = top_idx.reshape(-1)
    onehot = (e_flat[:, None] == jnp.arange(N_EXPERTS, dtype=jnp.int32)[None, :]).astype(jnp.int32)
    counts = jnp.sum(onehot, axis=0)
    rank = jnp.sum((jnp.cumsum(onehot, axis=0) - onehot) * onehot, axis=1)
    tiles = (counts + GROUP_TILE - 1) // GROUP_TILE
    tile_end = jnp.cumsum(tiles)
    tile_start = tile_end - tiles
    dest = ((tile_start * GROUP_TILE)[e_flat] + rank).astype(jnp.int32)
    t = jnp.arange(n_tiles, dtype=jnp.int32)
    used = tile_end[-1]
    t_eff = jnp.minimum(t, used - 1)
    tile_expert = jnp.sum((t_eff[:, None] >= tile_end[None, :]).astype(jnp.int32), axis=1)
    tile_valid = (t < used).astype(jnp.int32)
    prev_expert = jnp.concatenate([jnp.full((1,), -1, jnp.int32), tile_expert[:-1].astype(jnp.int32)])
    tile_first = (tile_expert != prev_expert).astype(jnp.int32)
    src = jnp.zeros((n_tiles * GROUP_TILE,), jnp.int32).at[dest].set(
        jnp.arange(e_flat.shape[0], dtype=jnp.int32) // TOP_K, unique_indices=True)
    return dest, tile_expert.astype(jnp.int32), t_eff.astype(jnp.int32), tile_valid, tile_first, src


def _gather_kernel(src_ref, tv_ref, x_hbm, o_ref, buf, sem):
    i = pl.program_id(0)
    rows = buf.shape[0]

    def copy(r):
        return pltpu.make_async_copy(x_hbm.at[pl.ds(src_ref[i * rows + r], 1)], buf.at[pl.ds(r, 1)], sem)

    def start(r, c):
        copy(r).start()
        return c

    def wait(r, c):
        copy(r).wait()
        return c

    @pl.when(tv_ref[i] == 1)
    def _():
        lax.fori_loop(0, rows, start, 0, unroll=8)
        lax.fori_loop(0, rows, wait, 0, unroll=8)
        o_ref[...] = buf[...].astype(o_ref.dtype)

    @pl.when(tv_ref[i] == 0)
    def _():
        o_ref[...] = jnp.zeros_like(o_ref)


def _gather_rows(x, plan, n_tiles):
    n = x.shape[1]
    return pl.pallas_call(
        _gather_kernel,
        grid_spec=pltpu.PrefetchScalarGridSpec(
            num_scalar_prefetch=2, grid=(n_tiles,),
            in_specs=[pl.BlockSpec(memory_space=pl.ANY)],
            out_specs=pl.BlockSpec((GROUP_TILE, n), lambda i, src, tv: (i, 0)),
            scratch_shapes=[pltpu.VMEM((GROUP_TILE, n), x.dtype), pltpu.SemaphoreType.DMA(())]),
        out_shape=jax.ShapeDtypeStruct((n_tiles * GROUP_TILE, n), BF16),
        compiler_params=_cparams(("arbitrary",)), name="moe_gather")(plan[5], plan[3], x)


def _moe_gateup_kernel(te_ref, tr_ref, tv_ref, tf_ref, x_ref, wg_ref, wu_ref, h_ref, wgb_ref, wub_ref):
    del te_ref, tr_ref
    i = pl.program_id(1)

    @pl.when(tf_ref[i] == 1)
    def _():
        wgb_ref[...] = wg_ref[...].astype(BF16)
        wub_ref[...] = wu_ref[...].astype(BF16)

    @pl.when(tv_ref[i] == 1)
    def _():
        x = x_ref[...]
        a = _dot(x, wgb_ref[...])
        u = _dot(x, wub_ref[...])
        h_ref[...] = (a * (1.0 / (1.0 + jnp.exp(-a))) * u).astype(h_ref.dtype)

    @pl.when(tv_ref[i] == 0)
    def _():
        h_ref[...] = jnp.zeros_like(h_ref)


def _moe_gate_up(xs, wg, wu, plan, n_tiles):
    _, tile_expert, tile_row, tile_valid, tile_first, _ = plan
    kdim = xs.shape[1]
    n = wg.shape[2]
    w_spec = pl.BlockSpec((None, kdim, FF_TILE), lambda j, i, te, tr, tv, tf: (te[i], 0, j))
    return pl.pallas_call(
        _moe_gateup_kernel,
        grid_spec=pltpu.PrefetchScalarGridSpec(
            num_scalar_prefetch=4, grid=(n // FF_TILE, n_tiles),
            in_specs=[pl.BlockSpec((GROUP_TILE, kdim), lambda j, i, te, tr, tv, tf: (tr[i], 0)), w_spec, w_spec],
            out_specs=pl.BlockSpec((GROUP_TILE, FF_TILE), lambda j, i, te, tr, tv, tf: (i, j)),
            scratch_shapes=[pltpu.VMEM((kdim, FF_TILE), BF16), pltpu.VMEM((kdim, FF_TILE), BF16)]),
        out_shape=jax.ShapeDtypeStruct((n_tiles * GROUP_TILE, n), BF16),
        compiler_params=_cparams(("arbitrary", "arbitrary")), name="moe_gate_up",
    )(tile_expert, tile_row, tile_valid, tile_first, xs, wg, wu)


def _moe_down_kernel(te_ref, tr_ref, tv_ref, h_ref, w_ref, y_ref, acc_ref):
    del te_ref, tr_ref
    i = pl.program_id(0)
    k = pl.program_id(1)

    @pl.when(k == 0)
    def _():
        acc_ref[...] = jnp.zeros_like(acc_ref)

    @pl.when(tv_ref[i] == 1)
    def _():
        acc_ref[...] += _dot(h_ref[...], w_ref[...].astype(BF16))

    @pl.when(k == pl.num_programs(1) - 1)
    def _():
        y_ref[...] = acc_ref[...]


def _moe_down(h, wd, plan, n_tiles):
    _, tile_expert, tile_row, tile_valid, _, _ = plan
    kdim = h.shape[1]
    n = wd.shape[2]
    tk = min(MOE_DOWN_K_TILE, kdim)
    return pl.pallas_call(
        _moe_down_kernel,
        grid_spec=pltpu.PrefetchScalarGridSpec(
            num_scalar_prefetch=3, grid=(n_tiles, kdim // tk),
            in_specs=[pl.BlockSpec((GROUP_TILE, tk), lambda i, k, te, tr, tv: (tr[i], k)),
                      pl.BlockSpec((None, tk, n),
                                   lambda i, k, te, tr, tv: (te[i], k * tv[i] + (kdim // tk - 1) * (1 - tv[i]), 0))],
            out_specs=pl.BlockSpec((GROUP_TILE, n), lambda i, k, te, tr, tv: (i, 0)),
            scratch_shapes=[pltpu.VMEM((GROUP_TILE, n), F32)]),
        out_shape=jax.ShapeDtypeStruct((n_tiles * GROUP_TILE, n), F32),
        compiler_params=_cparams(("arbitrary", "arbitrary")), name="moe_down",
    )(tile_expert, tile_row, tile_valid, h, wd)


def _combine_ln_kernel(dest_ref, ys_hbm, res_ref, gate_ref, g_ref, b_ref, y_ref, yb_ref, buf, sem,
                       *, tokens_per_step, n_tokens):
    base = pl.program_id(0) * tokens_per_step

    def copy(t, s):
        tok = jnp.minimum(base + t, n_tokens - 1)
        return pltpu.make_async_copy(
            ys_hbm.at[pl.ds(dest_ref[tok * TOP_K + s], 1)], buf.at[s, pl.ds(t, 1)], sem)

    def start(t, c):
        for s in range(TOP_K):
            copy(t, s).start()
        return c

    def wait(t, c):
        for s in range(TOP_K):
            copy(t, s).wait()
        return c

    lax.fori_loop(0, tokens_per_step, start, 0)
    lax.fori_loop(0, tokens_per_step, wait, 0)
    gate = gate_ref[...]
    f = gate[:, 0:1] * buf[0] + gate[:, 1:2] * buf[1]
    y = _layer_norm_rows(ALPHA * res_ref[...] + f, g_ref[...], b_ref[...])
    y_ref[...] = y
    yb_ref[...] = y.astype(BF16)


def _combine_residual_ln(ys, dest, gates, res, g, b, n_tokens):
    m, n = res.shape
    tps = COMBINE_TILE
    row = lambda i, d: (i, 0)
    vec = pl.BlockSpec((1, n), lambda i, d: (0, 0))
    return pl.pallas_call(
        functools.partial(_combine_ln_kernel, tokens_per_step=tps, n_tokens=n_tokens),
        grid_spec=pltpu.PrefetchScalarGridSpec(
            num_scalar_prefetch=1, grid=(m // tps,),
            in_specs=[pl.BlockSpec(memory_space=pl.ANY),
                      pl.BlockSpec((tps, n), row),
                      pl.BlockSpec((tps, LANES), row), vec, vec],
            out_specs=[pl.BlockSpec((tps, n), row), pl.BlockSpec((tps, n), row)],
            scratch_shapes=[pltpu.VMEM((TOP_K, tps, n), F32), pltpu.SemaphoreType.DMA(())]),
        out_shape=[jax.ShapeDtypeStruct((m, n), F32), jax.ShapeDtypeStruct((m, n), BF16)],
        compiler_params=_cparams(("arbitrary",)), name="moe_combine_ln",
    )(dest, ys, res, gates, g.reshape(1, n), b.reshape(1, n))


def _moe_block(x, xb, n_tokens, w_router, wg, wu, wd, g, b):
    del xb
    top_idx, gates = _route(x, w_router)
    n_rows = n_tokens * TOP_K
    n_tiles = -(-n_rows // GROUP_TILE) + N_EXPERTS
    plan = _routing_plan(top_idx[:n_tokens, :TOP_K], n_tiles)
    xs = _gather_rows(x, plan, n_tiles)
    h = _moe_gate_up(xs, wg, wu, plan, n_tiles)
    ys = _moe_down(h, wd, plan, n_tiles)
    return _combine_residual_ln(ys, plan[0], gates, x, g, b, n_tokens)


def _stack_heads(q):
    return jnp.concatenate([q[:, h * HEAD_DIM:(h + 1) * HEAD_DIM] for h in range(KV_GROUP)], axis=0)


def _store_heads(o_ref, o):
    blk = o.shape[0] // KV_GROUP
    for h in range(KV_GROUP):
        o_ref[:, h * HEAD_DIM:(h + 1) * HEAD_DIM] = o[h * blk:(h + 1) * blk].astype(o_ref.dtype)


def _topk_lane_mask(gate, n_cand, n_valid):
    lane = lax.broadcasted_iota(jnp.int32, gate.shape, 1)
    cnt = jnp.zeros(gate.shape, jnp.int32)
    for c in range(n_cand):
        col = gate[:, c:c + 1]
        beats = (col > gate) | ((col == gate) & (c < lane))
        cnt = cnt + jnp.where(beats, jnp.where(c < n_valid, 1, 0), 0)
    return (lane < n_valid) & (cnt < MOBA_TOPK)


def _row_chunk(c):
    return pl.ds(pl.multiple_of(c * ROW_CHUNK, ROW_CHUNK), ROW_CHUNK)


def _chunk_positions(c, width):
    q_pos = (c * ROW_CHUNK + lax.broadcasted_iota(jnp.int32, (ROW_CHUNK, width), 0)) & (ATT_BLOCK - 1)
    return q_pos, lax.broadcasted_iota(jnp.int32, (ROW_CHUNK, width), 1)


def _lane_tile(x, width):
    return jnp.concatenate([x] * (width // LANES), axis=1) if width > LANES else x


def _moba_prompt_kernel(q_ref, k_ref, v_ref, o_ref, kmean_ref, qb_ref, gate_ref, s_ref, p_ref,
                        m_ref, l_ref, a_ref, acc_ref, *, n_blk):
    qi = pl.program_id(2)
    blk = ATT_BLOCK
    n_chunks = KV_GROUP * blk // ROW_CHUNK

    @pl.when(qi == 0)
    def _():
        kmean_ref[...] = jnp.zeros_like(kmean_ref)
        for j in range(n_blk):
            kmean_ref[j:j + 1, :] = jnp.mean(k_ref[j * blk:(j + 1) * blk, :], axis=0, keepdims=True)

    qr = _stack_heads(q_ref[...])
    qb_ref[...] = qr.astype(BF16)
    gate_ref[...] = _dot_f32(qr, kmean_ref[...], nt=True)
    m_ref[...] = jnp.full(m_ref.shape, NEG_BIG, F32)
    l_ref[...] = jnp.zeros_like(l_ref)
    acc_ref[...] = jnp.zeros_like(acc_ref)

    def select(c, carry):
        r = _row_chunk(c)
        gate_ref[r, :] = _topk_lane_mask(gate_ref[r, :], n_blk - 1, qi).astype(F32)
        return carry

    lax.fori_loop(0, n_chunks, select, 0, unroll=CHUNK_UNROLL)
    lane_blk = lax.broadcasted_iota(jnp.int32, (ROW_CHUNK, LANES), 1)

    def attend(j, own):
        start = pl.multiple_of(j * blk, blk)
        s_ref[...] = _dot_nt(qb_ref[...], k_ref[pl.ds(start, blk), :].astype(BF16))

        def chunk(c, carry):
            r = _row_chunk(c)
            s = s_ref[r, :] * ATTN_SCALE
            if own:
                q_pos, k_pos = _chunk_positions(c, blk)
                s = jnp.where(k_pos <= q_pos, s, NEG_BIG)
            else:
                picked = jnp.max(jnp.where(lane_blk == j, gate_ref[r, :], 0.0), axis=1, keepdims=True) > 0.5
                s = jnp.where(picked, s, NEG_BIG)
            m_old = m_ref[r, :]
            m_new = jnp.maximum(m_old, jnp.max(s, axis=1, keepdims=True))
            a = jnp.exp(m_old - m_new)
            p = jnp.exp(s - _lane_tile(m_new, blk))
            l_ref[r, :] = a * l_ref[r, :] + jnp.sum(p, axis=1, keepdims=True)
            m_ref[r, :] = m_new
            a_ref[r, :] = a
            p_ref[r, :] = p.astype(BF16)
            return carry

        lax.fori_loop(0, n_chunks, chunk, 0, unroll=CHUNK_UNROLL)
        acc_ref[...] = a_ref[...] * acc_ref[...] + _dot(p_ref[...], v_ref[pl.ds(start, blk), :].astype(BF16))

    attend(qi, True)

    def past(j, carry):
        attend(j, False)
        return carry

    lax.fori_loop(0, qi, past, 0)
    _store_heads(o_ref, acc_ref[...] / l_ref[...])


def _moba_prompt(q, k, v, batch, seq):
    n_q = seq // ATT_BLOCK
    gw = KV_GROUP * HEAD_DIM
    rows = KV_GROUP * ATT_BLOCK
    kv_spec = pl.BlockSpec((seq, HEAD_DIM), lambda b, g, i: (b, g))
    return pl.pallas_call(
        functools.partial(_moba_prompt_kernel, n_blk=n_q),
        grid=(batch, N_KV_HEADS, n_q),
        in_specs=[pl.BlockSpec((ATT_BLOCK, gw), lambda b, g, i: (b * n_q + i, g)), kv_spec, kv_spec],
        out_specs=pl.BlockSpec((ATT_BLOCK, gw), lambda b, g, i: (b * n_q + i, g)),
        out_shape=jax.ShapeDtypeStruct((batch * seq, D_MODEL), BF16),
        scratch_shapes=[pltpu.VMEM((LANES, HEAD_DIM), F32),
                        pltpu.VMEM((rows, HEAD_DIM), BF16),
                        pltpu.VMEM((rows, LANES), F32),
                        pltpu.VMEM((rows, ATT_BLOCK), F32),
                        pltpu.VMEM((rows, ATT_BLOCK), BF16),
                        pltpu.VMEM((rows, LANES), F32), pltpu.VMEM((rows, LANES), F32),
                        pltpu.VMEM((rows, LANES), F32),
                        pltpu.VMEM((rows, HEAD_DIM), F32)],
        compiler_params=_cparams(("parallel", "parallel", "arbitrary")), name="moba_prompt")(q, k, v)


def _log_sigmoid_pair(z):
    t = jnp.log1p(jnp.exp(-jnp.abs(z)))
    return jnp.minimum(z, 0.0) - t, -jnp.maximum(z, 0.0) - t


def _suffix_sums(x, upper):
    hi = x.astype(BF16)
    lo = (x - hi.astype(F32)).astype(BF16)
    return _dot(hi, upper) + _dot(lo, upper)


def _strict_upper(n):
    r = lax.broadcasted_iota(jnp.int32, (n, n), 0)
    c = lax.broadcasted_iota(jnp.int32, (n, n), 1)
    return jnp.where(r > c, 1.0, 0.0).astype(BF16)


def _still_alive(carry_ref):
    return (jnp.max(carry_ref[...]) >= EXP_ZERO_BELOW).astype(jnp.int32)


def _sb_prompt_kernel(q_ref, k_ref, v_ref, o_ref, qb_ref, z_ref, lat_ref, hi_ref, lo_ref, w_ref,
                      carry_ref, rs_ref, acc_ref):
    qi = pl.program_id(2)
    blk = ATT_BLOCK
    n_chunks = KV_GROUP * blk // ROW_CHUNK
    qb_ref[...] = _stack_heads(q_ref[...]).astype(BF16)
    carry_ref[...] = jnp.zeros_like(carry_ref)
    acc_ref[...] = jnp.zeros_like(acc_ref)
    upper = _strict_upper(blk)

    def attend(j, diagonal):
        start = pl.multiple_of(j * blk, blk)
        z_ref[...] = _dot_nt(qb_ref[...], k_ref[pl.ds(start, blk), :].astype(BF16))

        def causal(c):
            q_pos, k_pos = _chunk_positions(c, blk)
            return k_pos < q_pos

        def keep_terms(c, carry):
            r = _row_chunk(c)
            log_beta, log_keep = _log_sigmoid_pair(z_ref[r, :] * ATTN_SCALE)
            if diagonal:
                log_keep = jnp.where(causal(c), log_keep, 0.0)
            hi = log_keep.astype(BF16)
            hi_ref[r, :] = hi
            lo_ref[r, :] = (log_keep - hi.astype(F32)).astype(BF16)
            z_ref[r, :] = log_beta + _lane_tile(carry_ref[r, :], blk)
            rs_ref[r, :] = jnp.broadcast_to(jnp.sum(log_keep, axis=1, keepdims=True), (ROW_CHUNK, LANES))
            return carry

        lax.fori_loop(0, n_chunks, keep_terms, 0, unroll=CHUNK_UNROLL)
        lat_ref[...] = _dot(hi_ref[...], upper) + _dot(lo_ref[...], upper)

        def weights(c, carry):
            r = _row_chunk(c)
            w = jnp.exp(z_ref[r, :] + lat_ref[r, :])
            if diagonal:
                w = jnp.where(causal(c), w, 0.0)
            w_ref[r, :] = w.astype(BF16)
            return carry

        lax.fori_loop(0, n_chunks, weights, 0, unroll=CHUNK_UNROLL)
        acc_ref[...] += _dot(w_ref[...], v_ref[pl.ds(start, blk), :].astype(BF16))
        carry_ref[...] += rs_ref[...]

    attend(qi, True)

    def more(state):
        t, alive = state
        return (t < qi) & (alive == 1)

    def older(state):
        t, _ = state
        attend(qi - 1 - t, False)
        return t + 1, _still_alive(carry_ref)

    lax.while_loop(more, older, (jnp.int32(0), _still_alive(carry_ref)))
    _store_heads(o_ref, acc_ref[...])


def _sb_prompt(q, k, v, batch, seq):
    n_q = seq // ATT_BLOCK
    gw = KV_GROUP * HEAD_DIM
    rows = KV_GROUP * ATT_BLOCK
    kv_spec = pl.BlockSpec((seq, HEAD_DIM), lambda b, g, i: (b, g))
    return pl.pallas_call(
        _sb_prompt_kernel, grid=(batch, N_KV_HEADS, n_q),
        in_specs=[pl.BlockSpec((ATT_BLOCK, gw), lambda b, g, i: (b * n_q + i, g)), kv_spec, kv_spec],
        out_specs=pl.BlockSpec((ATT_BLOCK, gw), lambda b, g, i: (b * n_q + i, g)),
        out_shape=jax.ShapeDtypeStruct((batch * seq, D_MODEL), BF16),
        scratch_shapes=[pltpu.VMEM((rows, HEAD_DIM), BF16),
                        pltpu.VMEM((rows, ATT_BLOCK), F32),
                        pltpu.VMEM((rows, ATT_BLOCK), F32),
                        pltpu.VMEM((rows, ATT_BLOCK), BF16), pltpu.VMEM((rows, ATT_BLOCK), BF16),
                        pltpu.VMEM((rows, ATT_BLOCK), BF16),
                        pltpu.VMEM((rows, LANES), F32), pltpu.VMEM((rows, LANES), F32),
                        pltpu.VMEM((rows, HEAD_DIM), F32)],
        compiler_params=_cparams(("parallel", "parallel", "arbitrary")), name="sb_prompt")(q, k, v)


def _group_rows(q, dec_batch, dec_seq):
    q = q.reshape(dec_batch, dec_seq, N_KV_HEADS, KV_GROUP, HEAD_DIM)
    return q.transpose(0, 2, 1, 3, 4).reshape(dec_batch, N_KV_HEADS, dec_seq * KV_GROUP, HEAD_DIM)


def _ungroup_rows(o, dec_batch, dec_seq):
    o = o.reshape(dec_batch, N_KV_HEADS, dec_seq, KV_GROUP, HEAD_DIM)
    return o.transpose(0, 2, 1, 3, 4).reshape(dec_batch * dec_seq, D_MODEL)


def _pad_new(x, dec_batch, dec_seq):
    x = x.reshape(dec_batch, dec_seq, KV_WIDTH)
    return jnp.pad(x, ((0, 0), (0, PAGE_SIZE - dec_seq), (0, 0)))


def _page_pair(ref0, ref1, g):
    return jnp.concatenate([ref0[:, g, :], ref1[:, g, :]], axis=0).astype(BF16)


def _moba_sample_kernel(pt_ref, q_ref, kn_ref, vn_ref, *refs, n_blk, dec_seq):
    del pt_ref
    pages = PAGES_PER_STEP
    k_refs, v_refs = refs[:pages], refs[pages:2 * pages]
    o_ref, s_ref, p_ref, kmean_ref, acc_ref, l_ref = refs[2 * pages:]
    step = pl.program_id(1)
    rg = dec_seq * KV_GROUP
    bps = pages // 2
    k_steps = n_blk // bps

    @pl.when(step == 0)
    def _():
        kmean_ref[...] = jnp.zeros_like(kmean_ref)

    @pl.when(step < k_steps)
    def _():
        for g in range(N_KV_HEADS):
            qb = q_ref[g].astype(BF16)
            for bl in range(bps):
                j = step * bps + bl
                kf = jnp.concatenate([k_refs[2 * bl][:, g, :], k_refs[2 * bl + 1][:, g, :]], axis=0)
                kmean_ref[g, pl.ds(j, 1), :] = jnp.mean(kf, axis=0, keepdims=True)
                s_ref[j, g * rg:(g + 1) * rg, :] = _dot_nt(qb, kf.astype(BF16)) * ATTN_SCALE

    @pl.when(step == k_steps)
    def _():
        sel_g, s_new_g = [], []
        tok = lax.broadcasted_iota(jnp.int32, (rg, PAGE_SIZE), 0) // KV_GROUP
        key = lax.broadcasted_iota(jnp.int32, (rg, PAGE_SIZE), 1)
        for g in range(N_KV_HEADS):
            gate = _dot_f32(q_ref[g], kmean_ref[g], nt=True)
            sel_g.append(_topk_lane_mask(gate, n_blk, n_blk).astype(F32))
            kn = kn_ref[:, g * HEAD_DIM:(g + 1) * HEAD_DIM].astype(BF16)
            s_new = _dot_nt(q_ref[g].astype(BF16), kn) * ATTN_SCALE
            s_new_g.append(jnp.where(key <= tok, s_new, NEG_BIG))
        sel = jnp.concatenate(sel_g, axis=0)
        s_new = jnp.concatenate(s_new_g, axis=0)
        m = jnp.max(s_new, axis=1, keepdims=True)
        for j in range(n_blk):
            picked = sel[:, j:j + 1] > 0.5
            m = jnp.maximum(m, jnp.max(jnp.where(picked, s_ref[j], NEG_BIG), axis=1, keepdims=True))
        p_new = jnp.exp(s_new - m)
        l = jnp.sum(p_new, axis=1, keepdims=True)
        for j in range(n_blk):
            picked = sel[:, j:j + 1] > 0.5
            p = jnp.where(picked, jnp.exp(s_ref[j] - m), 0.0)
            l = l + jnp.sum(p, axis=1, keepdims=True)
            p_ref[j] = p.astype(BF16)
        l_ref[...] = l
        for g in range(N_KV_HEADS):
            vn = vn_ref[:, g * HEAD_DIM:(g + 1) * HEAD_DIM].astype(BF16)
            acc_ref[g * rg:(g + 1) * rg, :] = _dot(p_new[g * rg:(g + 1) * rg].astype(BF16), vn)

    @pl.when(step >= k_steps)
    def _():
        for g in range(N_KV_HEADS):
            part = None
            for bl in range(bps):
                j = (step - k_steps) * bps + bl
                t = _dot(p_ref[j, g * rg:(g + 1) * rg, :], _page_pair(v_refs[2 * bl], v_refs[2 * bl + 1], g))
                part = t if part is None else part + t
            acc_ref[g * rg:(g + 1) * rg, :] += part

    @pl.when(step == 2 * k_steps - 1)
    def _():
        out = acc_ref[...] / l_ref[...]
        for g in range(N_KV_HEADS):
            o_ref[g] = out[g * rg:(g + 1) * rg]


def _moba_sample(qg, k_new, v_new, cache_k, cache_v, page_table, dec_seq):
    dec_batch, n_pages = page_table.shape
    n_blk = n_pages // 2
    rg = dec_seq * KV_GROUP
    rows = N_KV_HEADS * rg
    pages = PAGES_PER_STEP
    assert n_pages % pages == 0
    k_steps = n_pages // pages
    page = (None, PAGE_SIZE, N_KV_HEADS, HEAD_DIM)

    def k_map(off):
        return lambda b, s, pt: (pt[b, pages * jnp.minimum(s, k_steps - 1) + off], 0, 0, 0)

    def v_map(off):
        return lambda b, s, pt: (pt[b, pages * jnp.maximum(s - k_steps, 0) + off], 0, 0, 0)

    per_b4 = pl.BlockSpec((None, N_KV_HEADS, rg, HEAD_DIM), lambda b, s, pt: (b, 0, 0, 0))
    per_b3 = pl.BlockSpec((None, PAGE_SIZE, KV_WIDTH), lambda b, s, pt: (b, 0, 0))
    return pl.pallas_call(
        functools.partial(_moba_sample_kernel, n_blk=n_blk, dec_seq=dec_seq),
        grid_spec=pltpu.PrefetchScalarGridSpec(
            num_scalar_prefetch=1, grid=(dec_batch, 2 * k_steps),
            in_specs=[per_b4, per_b3, per_b3]
            + [pl.BlockSpec(page, k_map(i)) for i in range(pages)]
            + [pl.BlockSpec(page, v_map(i)) for i in range(pages)],
            out_specs=per_b4,
            scratch_shapes=[pltpu.VMEM((n_blk, rows, 2 * PAGE_SIZE), F32),
                            pltpu.VMEM((n_blk, rows, 2 * PAGE_SIZE), BF16),
                            pltpu.VMEM((N_KV_HEADS, LANES, HEAD_DIM), F32),
                            pltpu.VMEM((rows, HEAD_DIM), F32),
                            pltpu.VMEM((rows, 1), F32)]),
        out_shape=jax.ShapeDtypeStruct((dec_batch, N_KV_HEADS, rg, HEAD_DIM), F32),
        compiler_params=_cparams(("parallel", "arbitrary")), name="moba_sample",
    )(page_table, qg, k_new, v_new, *([cache_k] * pages), *([cache_v] * pages))


def _sb_sample_kernel(pt_ref, q_ref, kn_ref, vn_ref, *refs, dec_seq):
    del pt_ref
    pages = PAGES_PER_STEP
    k_refs, v_refs = refs[:pages], refs[pages:2 * pages]
    o_ref, acc_ref, carry_ref, alive_ref = refs[2 * pages:]
    step = pl.program_id(1)
    rg = dec_seq * KV_GROUP
    blk = 2 * PAGE_SIZE

    def accumulate(g, kb, vb, upper, causal):
        rows = slice(g * rg, (g + 1) * rg)
        z = _dot_nt(q_ref[g].astype(BF16), kb) * ATTN_SCALE
        log_beta, log_keep = _log_sigmoid_pair(z)
        if causal is not None:
            log_keep = jnp.where(causal, log_keep, 0.0)
        w = jnp.exp(log_beta + _suffix_sums(log_keep, upper) + carry_ref[rows, :])
        if causal is not None:
            w = jnp.where(causal, w, 0.0)
        acc_ref[rows, :] += _dot(w.astype(BF16), vb)
        carry_ref[rows, :] += jnp.sum(log_keep, axis=1, keepdims=True)

    @pl.when(step == 0)
    def _():
        acc_ref[...] = jnp.zeros_like(acc_ref)
        carry_ref[...] = jnp.zeros_like(carry_ref)
        tok = lax.broadcasted_iota(jnp.int32, (rg, PAGE_SIZE), 0) // KV_GROUP
        key = lax.broadcasted_iota(jnp.int32, (rg, PAGE_SIZE), 1)
        upper = _strict_upper(PAGE_SIZE)
        for g in range(N_KV_HEADS):
            cols = slice(g * HEAD_DIM, (g + 1) * HEAD_DIM)
            accumulate(g, kn_ref[:, cols].astype(BF16), vn_ref[:, cols].astype(BF16), upper, key < tok)
        alive_ref[0] = _still_alive(carry_ref)

    @pl.when(alive_ref[0] == 1)
    def _():
        upper = _strict_upper(blk)
        for bl in reversed(range(pages // 2)):
            for g in range(N_KV_HEADS):
                accumulate(g, _page_pair(k_refs[2 * bl], k_refs[2 * bl + 1], g),
                           _page_pair(v_refs[2 * bl], v_refs[2 * bl + 1], g), upper, None)
        alive_ref[0] = _still_alive(carry_ref)

    @pl.when(step == pl.num_programs(1) - 1)
    def _():
        for g in range(N_KV_HEADS):
            o_ref[g] = acc_ref[g * rg:(g + 1) * rg, :]


def _sb_sample(qg, k_new, v_new, cache_k, cache_v, page_table, dec_seq):
    dec_batch, n_pages = page_table.shape
    rg = dec_seq * KV_GROUP
    rows = N_KV_HEADS * rg
    pages = PAGES_PER_STEP
    assert n_pages % pages == 0
    page = (None, PAGE_SIZE, N_KV_HEADS, HEAD_DIM)

    def page_map(off):
        return lambda b, s, pt: (pt[b, n_pages - pages * (s + 1) + off], 0, 0, 0)

    per_b4 = pl.BlockSpec((None, N_KV_HEADS, rg, HEAD_DIM), lambda b, s, pt: (b, 0, 0, 0))
    per_b3 = pl.BlockSpec((None, PAGE_SIZE, KV_WIDTH), lambda b, s, pt: (b, 0, 0))
    page_specs = [pl.BlockSpec(page, page_map(i)) for i in range(pages)]
    return pl.pallas_call(
        functools.partial(_sb_sample_kernel, dec_seq=dec_seq),
        grid_spec=pltpu.PrefetchScalarGridSpec(
            num_scalar_prefetch=1, grid=(dec_batch, n_pages // pages),
            in_specs=[per_b4, per_b3, per_b3] + page_specs + page_specs,
            out_specs=per_b4,
            scratch_shapes=[pltpu.VMEM((rows, HEAD_DIM), F32), pltpu.VMEM((rows, 1), F32),
                            pltpu.SMEM((1,), jnp.int32)]),
        out_shape=jax.ShapeDtypeStruct((dec_batch, N_KV_HEADS, rg, HEAD_DIM), F32),
        compiler_params=_cparams(("parallel", "arbitrary")), name="sb_sample",
    )(page_table, qg, k_new, v_new, *([cache_k] * pages), *([cache_v] * pages))


def _pool_kernel(x_ref, w_ref, scale_ref, o_ref, *, seq, chunk, counts):
    g = pl.program_id(1)
    halo = POOL_STATE + 1
    w = w_ref[...].astype(BF16)
    for c0 in range(0, seq, chunk):
        cur = x_ref[halo + c0:halo + c0 + chunk, :]
        tot, out = cur, None
        for d in range(1, max(POOL_WINDOWS)):
            tot = tot + x_ref[halo + c0 - d:halo + c0 - d + chunk, :]
            if d + 1 in POOL_WINDOWS:
                gi = POOL_WINDOWS.index(d + 1)
                mixed = tot / counts(gi, c0, chunk) - cur
                out = mixed if out is None else jnp.where(g == gi, mixed, out)
        o_ref[c0:c0 + chunk, :] = _dot(out.astype(BF16), w) * scale_ref[...]


def _pool_mixer(x_hist, w_pool, scale, seq, first_pos):
    batch = x_hist.shape[0]
    halo = POOL_STATE + 1
    chunk = min(seq, 512)

    def counts(gi, c0, rows):
        win = POOL_WINDOWS[gi]
        pos = first_pos + c0 + lax.broadcasted_iota(jnp.int32, (rows, 1), 0)
        return jnp.minimum(win, pos + 1).astype(F32)

    return pl.pallas_call(
        functools.partial(_pool_kernel, seq=seq, chunk=chunk, counts=counts),
        grid=(batch, len(POOL_WINDOWS)),
        in_specs=[pl.BlockSpec((None, halo + seq, POOL_GROUP), lambda b, g: (b, 0, g)),
                  pl.BlockSpec((None, POOL_GROUP, POOL_GROUP), lambda b, g: (g, 0, 0)),
                  pl.BlockSpec((1, POOL_GROUP), lambda b, g: (0, g))],
        out_specs=pl.BlockSpec((None, seq, POOL_GROUP), lambda b, g: (b, 0, g)),
        out_shape=jax.ShapeDtypeStruct((batch, seq, D_MODEL), F32),
        compiler_params=_cparams(("parallel", "parallel")), name="pool_mixer",
    )(x_hist, w_pool, scale.reshape(1, D_MODEL))


def _rope_tables(pos):
    half = ROT_DIM // 2
    inv = ROPE_THETA ** (-jnp.arange(0, ROT_DIM, 2, dtype=F32) / ROT_DIM)
    ang = pos.astype(F32)[:, None] * inv[None, :]
    cos, sin = jnp.cos(ang), jnp.sin(ang)
    n = pos.shape[0]
    ones = jnp.ones((n, HEAD_DIM - ROT_DIM), F32)
    zeros = jnp.zeros((n, HEAD_DIM - half), F32)
    a = jnp.concatenate([cos, cos, ones], axis=1)
    b = jnp.concatenate([-sin, zeros], axis=1)
    c = jnp.concatenate([jnp.zeros((n, half), F32), sin, jnp.zeros((n, HEAD_DIM - ROT_DIM), F32)], axis=1)
    return a, b, c


def kernel(x_prompt, x_sample, cache_k_l0, cache_v_l0, state_pool_l1, cache_k_l2, cache_v_l2, cache_k_l3, cache_v_l3, page_table, l0_wq, l0_wk, l0_wv, l0_wo, l0_ln1_g, l0_ln1_b, l0_ffn_wg, l0_ffn_wu, l0_ffn_wd, l0_ln2_g, l0_ln2_b, l1_pool_w, l1_pool_scale, l1_ln1_g, l1_ln1_b, l1_router, l1_moe_wg, l1_moe_wu, l1_moe_wd, l1_ln2_g, l1_ln2_b, l2_wq, l2_wk, l2_wv, l2_wo, l2_ln1_g, l2_ln1_b, l2_ffn_wg, l2_ffn_wu, l2_ffn_wd, l2_ln2_g, l2_ln2_b, l3_wq, l3_wk, l3_wv, l3_wo, l3_ln1_g, l3_ln1_b, l3_router, l3_moe_wg, l3_moe_wu, l3_moe_wd, l3_ln2_g, l3_ln2_b):
    batch, seq, _ = x_prompt.shape
    dec_batch, dec_seq, _ = x_sample.shape
    n_prompt = batch * seq
    n_sample = dec_batch * dec_seq
    n_tokens = n_prompt + n_sample
    m_pad = -(-n_tokens // ROW_TILE) * ROW_TILE
    assert m_pad % LN_ROW_TILE == 0 and seq % ATT_BLOCK == 0 and PAST_LEN % MOBA_BLOCK == 0
    assert PAST_LEN // MOBA_BLOCK == (PAST_LEN + dec_seq - 1) // MOBA_BLOCK and dec_seq <= PAGE_SIZE

    x = jnp.concatenate([x_prompt.reshape(n_prompt, D_MODEL), x_sample.reshape(n_sample, D_MODEL),
                         jnp.zeros((m_pad - n_tokens, D_MODEL), F32)], axis=0)
    xb = x.astype(BF16)
    pos = jnp.concatenate([jnp.tile(jnp.arange(seq, dtype=jnp.int32), batch),
                           jnp.tile(PAST_LEN + jnp.arange(dec_seq, dtype=jnp.int32), dec_batch),
                           jnp.zeros((m_pad - n_tokens,), jnp.int32)])
    rope = _rope_tables(pos)

    def split(y, width):
        p = y[:n_prompt].reshape(batch, seq, width)
        s = y[n_prompt:n_tokens].reshape(dec_batch, dec_seq, width)
        return p, s

    def attention_layer(kind, x, xb, w, caches):
        wq, wk, wv, wo = w
        q = _project(xb, wq, rope if kind == 0 else None)
        k = _project(xb, wk, rope if kind == 0 else None)
        v = _project(xb, wv)
        prompt_fn = _moba_prompt if kind == 0 else _sb_prompt
        sample_fn = _moba_sample if kind == 0 else _sb_sample
        o_p = prompt_fn(q, k, v, batch, seq)
        qg = _group_rows(q[n_prompt:n_tokens], dec_batch, dec_seq)
        o_s = sample_fn(qg, _pad_new(k[n_prompt:n_tokens], dec_batch, dec_seq),
                        _pad_new(v[n_prompt:n_tokens], dec_batch, dec_seq), caches[0], caches[1], page_table, dec_seq)
        o_s = _ungroup_rows(o_s, dec_batch, dec_seq).astype(BF16)
        o = jnp.concatenate([o_p, o_s, jnp.zeros((m_pad - n_tokens, D_MODEL), BF16)], axis=0)
        kp, ks = split(k, KV_WIDTH)
        vp, vs = split(v, KV_WIDTH)
        shape4 = lambda t: t.reshape(t.shape[0], t.shape[1], N_KV_HEADS, HEAD_DIM)
        return o, wo, (shape4(kp), shape4(vp), shape4(ks), shape4(vs))

    def pool_layer(x):
        xp, xs = split(x, D_MODEL)
        hist_p = jnp.concatenate([jnp.zeros((batch, POOL_STATE + 1, D_MODEL), F32), xp], axis=1)
        seq_s = -(-dec_seq // SUBLANES) * SUBLANES
        hist_s = jnp.concatenate([jnp.zeros((dec_batch, 1, D_MODEL), F32), state_pool_l1, xs,
                                  jnp.zeros((dec_batch, seq_s - dec_seq, D_MODEL), F32)], axis=1)
        h_p = _pool_mixer(hist_p, l1_pool_w, l1_pool_scale, seq, 0)
        h_s = _pool_mixer(hist_s, l1_pool_w, l1_pool_scale, seq_s, PAST_LEN)[:, :dec_seq]
        h = jnp.concatenate([h_p.reshape(n_prompt, D_MODEL), h_s.reshape(n_sample, D_MODEL),
                             jnp.zeros((m_pad - n_tokens, D_MODEL), F32)], axis=0)
        state_p = hist_p[:, -POOL_STATE:]
        state_s = jnp.concatenate([state_pool_l1, xs], axis=1)[:, -POOL_STATE:]
        return h, (state_p, state_s)

    def dense_ffn(x, xb, w, g, b):
        wg, wu, wd = w
        return _down_residual_ln(_gate_up(xb, wg, wu), wd, x, g, b)

    o, wo, kv0 = attention_layer(0, x, xb, (l0_wq, l0_wk, l0_wv, l0_wo), (cache_k_l0, cache_v_l0))
    x, xb = _down_residual_ln(o, wo, x, l0_ln1_g, l0_ln1_b)
    x, xb = dense_ffn(x, xb, (l0_ffn_wg, l0_ffn_wu, l0_ffn_wd), l0_ln2_g, l0_ln2_b)
    h, pool_state = pool_layer(x)
    x, xb = _residual_ln(x, h, l1_ln1_g, l1_ln1_b)
    x, xb = _moe_block(x, xb, n_tokens, l1_router, l1_moe_wg, l1_moe_wu, l1_moe_wd, l1_ln2_g, l1_ln2_b)
    o, wo, kv2 = attention_layer(2, x, xb, (l2_wq, l2_wk, l2_wv, l2_wo), (cache_k_l2, cache_v_l2))
    x, xb = _down_residual_ln(o, wo, x, l2_ln1_g, l2_ln1_b)
    x, xb = dense_ffn(x, xb, (l2_ffn_wg, l2_ffn_wu, l2_ffn_wd), l2_ln2_g, l2_ln2_b)
    o, wo, kv3 = attention_layer(0, x, xb, (l3_wq, l3_wk, l3_wv, l3_wo), (cache_k_l3, cache_v_l3))
    x, xb = _down_residual_ln(o, wo, x, l3_ln1_g, l3_ln1_b)
    x, xb = _moe_block(x, xb, n_tokens, l3_router, l3_moe_wg, l3_moe_wu, l3_moe_wd, l3_ln2_g, l3_ln2_b)

    y_prompt, y_sample = split(x, D_MODEL)
    return (y_prompt, y_sample, kv0[0], kv0[1], kv0[2], kv0[3], pool_state[0], pool_state[1],
            kv2[0], kv2[1], kv2[2], kv2[3], kv3[0], kv3[1], kv3[2], kv3[3])
```

```python
import functools
import math

import jax
import jax.numpy as jnp
from jax import lax
from jax.experimental import pallas as pl
from jax.experimental.pallas import tpu as pltpu

F32 = jnp.float32
BF16 = jnp.bfloat16

D_MODEL = 2048
DEPTH = 4
PAST_LEN = 16384
PAGE_SIZE = 128
N_HEADS = 16
N_KV_HEADS = 4
HEAD_DIM = D_MODEL // N_HEADS
KV_GROUP = N_HEADS // N_KV_HEADS
KV_WIDTH = N_KV_HEADS * HEAD_DIM
ROT_DIM = HEAD_DIM // 4
ROPE_THETA = 500000.0
MOBA_BLOCK = 256
MOBA_TOPK = 3
POOL_WINDOWS = (2, 4, 8, 16)
POOL_GROUP = D_MODEL // len(POOL_WINDOWS)
POOL_STATE = max(POOL_WINDOWS) - 1
N_EXPERTS = 8
TOP_K = 2
ALPHA = (2 * DEPTH) ** 0.25
LN_EPS = 1e-5
ATTN_SCALE = HEAD_DIM ** -0.5

LANES = 128
SUBLANES = 8
VMEM_LIMIT_BYTES = 56 * 1024 * 1024

ROW_TILE = 1056
LN_ROW_TILE = 528
FF_TILE = 512
GROUP_TILE = 512
SUPER_ROWS = 2048
MOE_FF_TILE = 256
MOE_DOWN_K_TILE = 512
MOE_DOWN_N_TILE = 1024
COMBINE_TILE = 264
DOWN_ROW_TILE = 1056
DOWN_K_TILE = 256
LN_CHUNK = 96
ROW_CHUNK = 64
CHUNK_UNROLL = 16
PAGES_PER_STEP = 8
ATT_BLOCK = 256
NEG_BIG = -1e30
EXP_ZERO_BELOW = -104.0


def _cparams(sem):
    return pltpu.CompilerParams(dimension_semantics=sem, vmem_limit_bytes=VMEM_LIMIT_BYTES)


def _dot(a, b):
    return jnp.dot(a, b, preferred_element_type=F32)


def _dot_nt(a, b):
    return lax.dot_general(a, b, (((1,), (1,)), ((), ())), preferred_element_type=F32)


def _layer_norm_rows(y, g, b):
    mu = jnp.mean(y, axis=-1, keepdims=True)
    d = y - mu
    var = jnp.mean(d * d, axis=-1, keepdims=True)
    return d * lax.rsqrt(var + LN_EPS) * g + b


def _proj_kernel(x_ref, w_ref, o_ref):
    o_ref[...] = _dot(x_ref[...].astype(BF16), w_ref[...].astype(BF16))


def _proj_rope_kernel(x_ref, w_ref, ca_ref, sb_ref, sc_ref, o_ref):
    y = _dot(x_ref[...].astype(BF16), w_ref[...].astype(BF16))
    half = ROT_DIM // 2
    ca, sb, sc = ca_ref[...], sb_ref[...], sc_ref[...]
    for h in range(y.shape[1] // HEAD_DIM):
        yh = y[:, h * HEAD_DIM:(h + 1) * HEAD_DIM]
        up = pltpu.roll(yh, HEAD_DIM - half, axis=1)
        down = pltpu.roll(yh, half, axis=1)
        o_ref[:, h * HEAD_DIM:(h + 1) * HEAD_DIM] = yh * ca + up * sb + down * sc


def _project(x, w, rope=None, col_tile=512):
    m, kdim = x.shape
    n = w.shape[1]
    tn = min(col_tile, n)
    grid = (m // ROW_TILE, n // tn)
    x_spec = pl.BlockSpec((ROW_TILE, kdim), lambda i, j: (i, 0))
    w_spec = pl.BlockSpec((kdim, tn), lambda i, j: (0, j))
    o_spec = pl.BlockSpec((ROW_TILE, tn), lambda i, j: (i, j))
    if rope is None:
        return pl.pallas_call(
            _proj_kernel, grid=grid, in_specs=[x_spec, w_spec], out_specs=o_spec,
            out_shape=jax.ShapeDtypeStruct((m, n), F32),
            compiler_params=_cparams(("parallel", "arbitrary")), name="project")(x, w)
    t_spec = pl.BlockSpec((ROW_TILE, HEAD_DIM), lambda i, j: (i, 0))
    return pl.pallas_call(
        _proj_rope_kernel, grid=grid, in_specs=[x_spec, w_spec, t_spec, t_spec, t_spec],
        out_specs=o_spec, out_shape=jax.ShapeDtypeStruct((m, n), F32),
        compiler_params=_cparams(("parallel", "arbitrary")), name="project_rope")(x, w, *rope)


def _gateup_kernel(x_ref, wg_ref, wu_ref, h_ref):
    x = x_ref[...].astype(BF16)
    a = _dot(x, wg_ref[...].astype(BF16))
    u = _dot(x, wu_ref[...].astype(BF16))
    h_ref[...] = (a * (1.0 / (1.0 + jnp.exp(-a))) * u).astype(h_ref.dtype)


def _gate_up(x, wg, wu):
    m, kdim = x.shape
    n = wg.shape[1]
    grid = (m // ROW_TILE, n // FF_TILE)
    return pl.pallas_call(
        _gateup_kernel, grid=grid,
        in_specs=[pl.BlockSpec((ROW_TILE, kdim), lambda i, j: (i, 0)),
                  pl.BlockSpec((kdim, FF_TILE), lambda i, j: (0, j)),
                  pl.BlockSpec((kdim, FF_TILE), lambda i, j: (0, j))],
        out_specs=pl.BlockSpec((ROW_TILE, FF_TILE), lambda i, j: (i, j)),
        out_shape=jax.ShapeDtypeStruct((m, n), BF16),
        compiler_params=_cparams(("parallel", "arbitrary")), name="gate_up")(x, wg, wu)


def _down_ln_kernel(h_ref, w_ref, res_ref, g_ref, b_ref, y_ref, yb_ref, acc_ref):
    k = pl.program_id(1)

    @pl.when(k == 0)
    def _():
        acc_ref[...] = jnp.zeros_like(acc_ref)

    acc_ref[...] += _dot(h_ref[...].astype(BF16), w_ref[...].astype(BF16))

    @pl.when(k == pl.num_programs(1) - 1)
    def _():
        rows = acc_ref.shape[0]
        chunk = math.gcd(rows, LN_CHUNK)

        def body(c, carry):
            r = pl.ds(pl.multiple_of(c * chunk, 2 * SUBLANES), chunk)
            y = _layer_norm_rows(ALPHA * res_ref[r, :] + acc_ref[r, :], g_ref[...], b_ref[...])
            y_ref[r, :] = y
            yb_ref[r, :] = y.astype(BF16)
            return carry

        lax.fori_loop(0, rows // chunk, body, 0)


def _down_residual_ln(h, w, res, g, b):
    m, kdim = h.shape
    n = w.shape[1]
    tm = DOWN_ROW_TILE
    tk = min(DOWN_K_TILE, kdim)
    grid = (m // tm, kdim // tk)
    row = lambda i, k: (i, 0)
    return pl.pallas_call(
        _down_ln_kernel, grid=grid,
        in_specs=[pl.BlockSpec((tm, tk), lambda i, k: (i, k)),
                  pl.BlockSpec((tk, n), lambda i, k: (k, 0)),
                  pl.BlockSpec((tm, n), row, pipeline_mode=pl.Buffered(1)),
                  pl.BlockSpec((1, n), lambda i, k: (0, 0)),
                  pl.BlockSpec((1, n), lambda i, k: (0, 0))],
        out_specs=[pl.BlockSpec((tm, n), row), pl.BlockSpec((tm, n), row)],
        out_shape=[jax.ShapeDtypeStruct((m, n), F32), jax.ShapeDtypeStruct((m, n), BF16)],
        scratch_shapes=[pltpu.VMEM((tm, n), F32)],
        compiler_params=_cparams(("parallel", "arbitrary")), name="down_residual_ln",
    )(h, w, res, g.reshape(1, n), b.reshape(1, n))


def _residual_ln_kernel(res_ref, h_ref, g_ref, b_ref, y_ref, yb_ref):
    y = _layer_norm_rows(ALPHA * res_ref[...] + h_ref[...], g_ref[...], b_ref[...])
    y_ref[...] = y
    yb_ref[...] = y.astype(BF16)


def _residual_ln(res, h, g, b):
    m, n = res.shape
    row = pl.BlockSpec((LN_ROW_TILE, n), lambda i: (i, 0))
    vec = pl.BlockSpec((1, n), lambda i: (0, 0))
    return pl.pallas_call(
        _residual_ln_kernel, grid=(m // LN_ROW_TILE,), in_specs=[row, row, vec, vec],
        out_specs=[row, row],
        out_shape=[jax.ShapeDtypeStruct((m, n), F32), jax.ShapeDtypeStruct((m, n), BF16)],
        compiler_params=_cparams(("parallel",)), name="residual_ln",
    )(res, h, g.reshape(1, n), b.reshape(1, n))


def _router_kernel(x_ref, w_ref, idx_ref, gate_ref):
    logits = _dot(x_ref[...].astype(BF16), w_ref[...].astype(BF16))
    lane = lax.broadcasted_iota(jnp.int32, logits.shape, 1)
    logits = jnp.where(lane < N_EXPERTS, logits, -jnp.inf)
    v1 = jnp.max(logits, axis=1, keepdims=True)
    i1 = jnp.min(jnp.where(logits == v1, lane, LANES), axis=1, keepdims=True)
    rest = jnp.where(lane == i1, -jnp.inf, logits)
    v2 = jnp.max(rest, axis=1, keepdims=True)
    i2 = jnp.min(jnp.where(rest == v2, lane, LANES), axis=1, keepdims=True)
    e = jnp.exp(v2 - v1)
    g1 = 1.0 / (1.0 + e)
    g2 = e / (1.0 + e)
    idx_ref[...] = jnp.where(lane == 0, i1, jnp.where(lane == 1, i2, 0))
    gate_ref[...] = jnp.where(lane == 0, g1, jnp.where(lane == 1, g2, 0.0))


def _route(x, w_router):
    m, kdim = x.shape
    w_pad = jnp.zeros((kdim, LANES), F32).at[:, :N_EXPERTS].set(w_router)
    out = pl.BlockSpec((LN_ROW_TILE, LANES), lambda i: (i, 0))
    return pl.pallas_call(
        _router_kernel, grid=(m // LN_ROW_TILE,),
        in_specs=[pl.BlockSpec((LN_ROW_TILE, kdim), lambda i: (i, 0)),
                  pl.BlockSpec((kdim, LANES), lambda i: (0, 0))],
        out_specs=[out, out],
        out_shape=[jax.ShapeDtypeStruct((m, LANES), jnp.int32), jax.ShapeDtypeStruct((m, LANES), F32)],
        compiler_params=_cparams(("parallel",)), name="router")(x, w_pad)


def _routing_plan(top_idx, n_sup):
    tiles_per_sup = SUPER_ROWS // GROUP_TILE
    e_flat = top_idx.reshape(-1)
    onehot = (e_flat[:, None] == jnp.arange(N_EXPERTS, dtype=jnp.int32)[None, :]).astype(jnp.int32)
    counts = jnp.sum(onehot, axis=0)
    rank = jnp.sum((jnp.cumsum(onehot, axis=0) - onehot) * onehot, axis=1)
    sups = (counts + SUPER_ROWS - 1) // SUPER_ROWS
    sup_end = jnp.cumsum(sups)
    sup_start = sup_end - sups
    dest = ((sup_start * SUPER_ROWS)[e_flat] + rank).astype(jnp.int32)
    s = jnp.arange(n_sup, dtype=jnp.int32)
    used = sup_end[-1]
    s_eff = jnp.minimum(s, used - 1).astype(jnp.int32)
    sup_expert = jnp.sum((s_eff[:, None] >= sup_end[None, :]).astype(jnp.int32), axis=1)
    rows_left = counts[sup_expert] - (s_eff - sup_start[sup_expert]) * SUPER_ROWS
    sup_tiles = jnp.clip((rows_left + GROUP_TILE - 1) // GROUP_TILE, 0, tiles_per_sup) * (s < used)
    src = jnp.zeros((n_sup * SUPER_ROWS,), jnp.int32).at[dest].set(
        jnp.arange(e_flat.shape[0], dtype=jnp.int32) // TOP_K, unique_indices=True)
    return dest, sup_expert.astype(jnp.int32), s_eff, sup_tiles.astype(jnp.int32), src


def _gather_kernel(st_ref, src_ref, x_hbm, o_ref, buf, sem):
    t = pl.program_id(0)
    rows = buf.shape[0]
    tiles_per_sup = SUPER_ROWS // rows
    populated = lax.rem(t, tiles_per_sup) < st_ref[lax.div(t, tiles_per_sup)]

    def copy(r):
        return pltpu.make_async_copy(x_hbm.at[pl.ds(src_ref[0, r], 1)], buf.at[pl.ds(r, 1)], sem)

    def start(r, c):
        copy(r).start()
        return c

    def wait(r, c):
        copy(r).wait()
        return c

    @pl.when(populated)
    def _():
        lax.fori_loop(0, rows, start, 0, unroll=8)
        lax.fori_loop(0, rows, wait, 0, unroll=8)
        o_ref[...] = buf[...].astype(o_ref.dtype)

    @pl.when(jnp.logical_not(populated))
    def _():
        o_ref[...] = jnp.zeros_like(o_ref)


def _gather_rows(x, plan, n_sup):
    n = x.shape[1]
    n_tiles = n_sup * (SUPER_ROWS // GROUP_TILE)
    src = plan[4].reshape(n_tiles, 1, GROUP_TILE)
    return pl.pallas_call(
        _gather_kernel,
        grid_spec=pltpu.PrefetchScalarGridSpec(
            num_scalar_prefetch=1, grid=(n_tiles,),
            in_specs=[pl.BlockSpec((None, 1, GROUP_TILE), lambda t, st: (t, 0, 0), memory_space=pltpu.SMEM),
                      pl.BlockSpec(memory_space=pl.ANY)],
            out_specs=pl.BlockSpec((GROUP_TILE, n), lambda t, st: (t, 0)),
            scratch_shapes=[pltpu.VMEM((GROUP_TILE, n), x.dtype), pltpu.SemaphoreType.DMA(())]),
        out_shape=jax.ShapeDtypeStruct((n_tiles * GROUP_TILE, n), BF16),
        compiler_params=_cparams(("arbitrary",)), name="moe_gather")(plan[3], src, x)


def _moe_gateup_kernel(se_ref, ss_ref, st_ref, x_ref, wg_ref, wu_ref, h_ref):
    del se_ref, ss_ref
    tiles = st_ref[pl.program_id(0)]

    for n in range(1, SUPER_ROWS // GROUP_TILE + 1):
        @pl.when(tiles == n)
        def _(rows=n * GROUP_TILE):
            x = x_ref[:rows, :]
            a = _dot(x, wg_ref[...].astype(BF16))
            u = _dot(x, wu_ref[...].astype(BF16))
            h_ref[:rows, :] = (a * (1.0 / (1.0 + jnp.exp(-a))) * u).astype(h_ref.dtype)
            if rows < SUPER_ROWS:
                h_ref[rows:, :] = jnp.zeros((SUPER_ROWS - rows, h_ref.shape[1]), h_ref.dtype)

    @pl.when(tiles == 0)
    def _():
        h_ref[...] = jnp.zeros_like(h_ref)


def _moe_gate_up(xs, wg, wu, plan, n_sup):
    _, sup_expert, sup_src, sup_tiles, _ = plan
    kdim = xs.shape[1]
    n = wg.shape[2]
    tn = min(MOE_FF_TILE, n)
    n_j = n // tn

    def w_map(s, j, se, ss, st):
        return se[s], 0, jnp.where(st[s] > 0, j, n_j - 1)

    w_spec = pl.BlockSpec((None, kdim, tn), w_map)
    return pl.pallas_call(
        _moe_gateup_kernel,
        grid_spec=pltpu.PrefetchScalarGridSpec(
            num_scalar_prefetch=3, grid=(n_sup, n_j),
            in_specs=[pl.BlockSpec((SUPER_ROWS, kdim), lambda s, j, se, ss, st: (ss[s], 0)), w_spec, w_spec],
            out_specs=pl.BlockSpec((SUPER_ROWS, tn), lambda s, j, se, ss, st: (s, j))),
        out_shape=jax.ShapeDtypeStruct((n_sup * SUPER_ROWS, n), BF16),
        compiler_params=_cparams(("arbitrary", "arbitrary")), name="moe_gate_up",
    )(sup_expert, sup_src, sup_tiles, xs, wg, wu)


def _moe_down_kernel(se_ref, ss_ref, st_ref, h_ref, w_ref, y_ref):
    del se_ref, ss_ref
    tiles = st_ref[pl.program_id(0)]
    k = pl.program_id(2)

    @pl.when(k == 0)
    def _():
        y_ref[...] = jnp.zeros_like(y_ref)

    for n in range(1, SUPER_ROWS // GROUP_TILE + 1):
        @pl.when(tiles == n)
        def _(rows=n * GROUP_TILE):
            y_ref[:rows, :] += _dot(h_ref[:rows, :], w_ref[...].astype(BF16))


def _moe_down(h, wd, plan, n_sup):
    _, sup_expert, sup_src, sup_tiles, _ = plan
    kdim = h.shape[1]
    n = wd.shape[2]
    tk = min(MOE_DOWN_K_TILE, kdim)
    tn = min(MOE_DOWN_N_TILE, n)
    n_k = kdim // tk

    def h_map(s, c, k, se, ss, st):
        return ss[s], jnp.where(st[s] > 0, k, n_k - 1)

    def w_map(s, c, k, se, ss, st):
        return se[s], jnp.where(st[s] > 0, k, n_k - 1), c

    return pl.pallas_call(
        _moe_down_kernel,
        grid_spec=pltpu.PrefetchScalarGridSpec(
            num_scalar_prefetch=3, grid=(n_sup, n // tn, n_k),
            in_specs=[pl.BlockSpec((SUPER_ROWS, tk), h_map), pl.BlockSpec((None, tk, tn), w_map)],
            out_specs=pl.BlockSpec((SUPER_ROWS, tn), lambda s, c, k, se, ss, st: (s, c))),
        out_shape=jax.ShapeDtypeStruct((n_sup * SUPER_ROWS, n), F32),
        compiler_params=_cparams(("arbitrary", "arbitrary", "arbitrary")), name="moe_down",
    )(sup_expert, sup_src, sup_tiles, h, wd)


def _combine_ln_kernel(dest_ref, ys_hbm, res_ref, gate_ref, g_ref, b_ref, y_ref, yb_ref, buf, sem,
                       *, tokens_per_step, n_tokens):
    base = pl.program_id(0) * tokens_per_step

    def copy(t, s):
        tok = jnp.minimum(base + t, n_tokens - 1)
        return pltpu.make_async_copy(
            ys_hbm.at[pl.ds(dest_ref[tok * TOP_K + s], 1)], buf.at[s, pl.ds(t, 1)], sem)

    def start(t, c):
        for s in range(TOP_K):
            copy(t, s).start()
        return c

    def wait(t, c):
        for s in range(TOP_K):
            copy(t, s).wait()
        return c

    lax.fori_loop(0, tokens_per_step, start, 0)
    lax.fori_loop(0, tokens_per_step, wait, 0)
    gate = gate_ref[...]
    f = gate[:, 0:1] * buf[0] + gate[:, 1:2] * buf[1]
    y = _layer_norm_rows(ALPHA * res_ref[...] + f, g_ref[...], b_ref[...])
    y_ref[...] = y
    yb_ref[...] = y.astype(BF16)


def _combine_residual_ln(ys, dest, gates, res, g, b, n_tokens):
    m, n = res.shape
    tps = COMBINE_TILE
    row = lambda i, d: (i, 0)
    vec = pl.BlockSpec((1, n), lambda i, d: (0, 0))
    return pl.pallas_call(
        functools.partial(_combine_ln_kernel, tokens_per_step=tps, n_tokens=n_tokens),
        grid_spec=pltpu.PrefetchScalarGridSpec(
            num_scalar_prefetch=1, grid=(m // tps,),
            in_specs=[pl.BlockSpec(memory_space=pl.ANY),
                      pl.BlockSpec((tps, n), row),
                      pl.BlockSpec((tps, LANES), row), vec, vec],
            out_specs=[pl.BlockSpec((tps, n), row), pl.BlockSpec((tps, n), row)],
            scratch_shapes=[pltpu.VMEM((TOP_K, tps, n), F32), pltpu.SemaphoreType.DMA(())]),
        out_shape=[jax.ShapeDtypeStruct((m, n), F32), jax.ShapeDtypeStruct((m, n), BF16)],
        compiler_params=_cparams(("arbitrary",)), name="moe_combine_ln",
    )(dest, ys, res, gates, g.reshape(1, n), b.reshape(1, n))


def _moe_block(x, xb, n_tokens, w_router, wg, wu, wd, g, b):
    del xb
    top_idx, gates = _route(x, w_router)
    n_sup = n_tokens * TOP_K // SUPER_ROWS + N_EXPERTS
    plan = _routing_plan(top_idx[:n_tokens, :TOP_K], n_sup)
    xs = _gather_rows(x, plan, n_sup)
    h = _moe_gate_up(xs, wg, wu, plan, n_sup)
    ys = _moe_down(h, wd, plan, n_sup)
    return _combine_residual_ln(ys, plan[0], gates, x, g, b, n_tokens)


def _stack_heads(q):
    return jnp.concatenate([q[:, h * HEAD_DIM:(h + 1) * HEAD_DIM] for h in range(KV_GROUP)], axis=0)


def _store_heads(o_ref, o):
    blk = o.shape[0] // KV_GROUP
    for h in range(KV_GROUP):
        o_ref[:, h * HEAD_DIM:(h + 1) * HEAD_DIM] = o[h * blk:(h + 1) * blk].astype(o_ref.dtype)


def _topk_lane_mask(gate, n_cand, n_valid):
    lane = lax.broadcasted_iota(jnp.int32, gate.shape, 1)
    cnt = jnp.zeros(gate.shape, jnp.int32)
    for c in range(n_cand):
        col = gate[:, c:c + 1]
        beats = (col > gate) | ((col == gate) & (c < lane))
        cnt = cnt + jnp.where(beats, jnp.where(c < n_valid, 1, 0), 0)
    return (lane < n_valid) & (cnt < MOBA_TOPK)


def _row_chunk(c):
    return pl.ds(pl.multiple_of(c * ROW_CHUNK, ROW_CHUNK), ROW_CHUNK)


def _chunk_positions(c, width):
    q_pos = (c * ROW_CHUNK + lax.broadcasted_iota(jnp.int32, (ROW_CHUNK, width), 0)) & (ATT_BLOCK - 1)
    return q_pos, lax.broadcasted_iota(jnp.int32, (ROW_CHUNK, width), 1)


def _lane_tile(x, width):
    return jnp.concatenate([x] * (width // LANES), axis=1) if width > LANES else x


def _moba_prompt_kernel(q_ref, k_ref, v_ref, o_ref, kmean_ref, qb_ref, gate_ref, bias_ref, s_ref, p_ref,
                        m_ref, l_ref, a_ref, acc_ref, *, n_blk):
    qi = pl.program_id(2)
    blk = ATT_BLOCK
    n_chunks = KV_GROUP * blk // ROW_CHUNK

    @pl.when(qi == 0)
    def _():
        kmean_ref[...] = jnp.zeros_like(kmean_ref)
        for j in range(n_blk):
            kmean_ref[j:j + 1, :] = jnp.mean(k_ref[j * blk:(j + 1) * blk, :], axis=0, keepdims=True)

    qr = _stack_heads(q_ref[...])
    qb_ref[...] = qr.astype(BF16)
    gate_ref[...] = _dot_nt(qr.astype(BF16), kmean_ref[...].astype(BF16))
    m_ref[...] = jnp.full(m_ref.shape, NEG_BIG, F32)
    l_ref[...] = jnp.zeros_like(l_ref)
    acc_ref[...] = jnp.zeros_like(acc_ref)

    def select(c, carry):
        r = _row_chunk(c)
        gate_ref[r, :] = _topk_lane_mask(gate_ref[r, :], n_blk - 1, qi).astype(F32)
        return carry

    lax.fori_loop(0, n_chunks, select, 0, unroll=CHUNK_UNROLL)

    n_past = n_blk - 1
    spread_row = lax.broadcasted_iota(jnp.int32, (LANES, n_past * LANES), 0)
    spread_col = lax.broadcasted_iota(jnp.int32, (LANES, n_past * LANES), 1)
    spread = jnp.where(spread_row == lax.shift_right_logical(spread_col, LANES.bit_length() - 1), 1.0, 0.0)
    replicated = _dot(gate_ref[...].astype(BF16), spread.astype(BF16))
    for j in range(n_past):
        bias_ref[j] = (replicated[:, j * LANES:(j + 1) * LANES] - 1.0) * (-NEG_BIG)
    ones = jnp.ones((blk, LANES), BF16)

    def attend(j, own):
        start = pl.multiple_of(j * blk, blk)
        s_ref[...] = _dot_nt(qb_ref[...], k_ref[pl.ds(start, blk), :].astype(BF16))

        def chunk(c, carry):
            r = _row_chunk(c)
            s = s_ref[r, :] * ATTN_SCALE
            if own:
                q_pos, k_pos = _chunk_positions(c, blk)
                s = jnp.where(k_pos <= q_pos, s, NEG_BIG)
            else:
                s = s + _lane_tile(bias_ref[j, r, :], blk)
            m_old = m_ref[r, :]
            m_new = jnp.maximum(m_old, jnp.max(s, axis=1, keepdims=True))
            m_ref[r, :] = m_new
            a_ref[r, :] = jnp.exp(m_old - m_new)
            p_ref[r, :] = jnp.exp(s - _lane_tile(m_new, blk)).astype(BF16)
            return carry

        lax.fori_loop(0, n_chunks, chunk, 0, unroll=CHUNK_UNROLL)
        v_ext = jnp.concatenate([v_ref[pl.ds(start, blk), :].astype(BF16), ones], axis=1)
        pv = _dot(p_ref[...], v_ext)
        a = a_ref[...]
        acc_ref[...] = a * acc_ref[...] + pv[:, :HEAD_DIM]
        l_ref[...] = a * l_ref[...] + pv[:, HEAD_DIM:]

    attend(qi, True)

    def past(j, carry):
        attend(j, False)
        return carry

    lax.fori_loop(0, qi, past, 0)
    _store_heads(o_ref, acc_ref[...] / l_ref[...])


def _moba_prompt(q, k, v, batch, seq):
    n_q = seq // ATT_BLOCK
    gw = KV_GROUP * HEAD_DIM
    rows = KV_GROUP * ATT_BLOCK
    kv_spec = pl.BlockSpec((seq, HEAD_DIM), lambda b, g, i: (b, g))
    return pl.pallas_call(
        functools.partial(_moba_prompt_kernel, n_blk=n_q),
        grid=(batch, N_KV_HEADS, n_q),
        in_specs=[pl.BlockSpec((ATT_BLOCK, gw), lambda b, g, i: (b * n_q + i, g)), kv_spec, kv_spec],
        out_specs=pl.BlockSpec((ATT_BLOCK, gw), lambda b, g, i: (b * n_q + i, g)),
        out_shape=jax.ShapeDtypeStruct((batch * seq, D_MODEL), BF16),
        scratch_shapes=[pltpu.VMEM((LANES, HEAD_DIM), F32),
                        pltpu.VMEM((rows, HEAD_DIM), BF16),
                        pltpu.VMEM((rows, LANES), F32),
                        pltpu.VMEM((n_q - 1, rows, LANES), F32),
                        pltpu.VMEM((rows, ATT_BLOCK), F32),
                        pltpu.VMEM((rows, ATT_BLOCK), BF16),
                        pltpu.VMEM((rows, LANES), F32), pltpu.VMEM((rows, LANES), F32),
                        pltpu.VMEM((rows, LANES), F32),
                        pltpu.VMEM((rows, HEAD_DIM), F32)],
        compiler_params=_cparams(("parallel", "parallel", "arbitrary")), name="moba_prompt")(q, k, v)


def _log_sigmoid_pair(z):
    t = jnp.log1p(jnp.exp(-jnp.abs(z)))
    return jnp.minimum(z, 0.0) - t, -jnp.maximum(z, 0.0) - t


def _suffix_sums(x, upper):
    hi = x.astype(BF16)
    lo = (x - hi.astype(F32)).astype(BF16)
    return _dot(hi, upper) + _dot(lo, upper)


def _strict_upper(n):
    r = lax.broadcasted_iota(jnp.int32, (n, n), 0)
    c = lax.broadcasted_iota(jnp.int32, (n, n), 1)
    return jnp.where(r > c, 1.0, 0.0).astype(BF16)


def _still_alive(carry_ref):
    return (jnp.max(carry_ref[...]) >= EXP_ZERO_BELOW).astype(jnp.int32)


def _sb_prompt_kernel(q_ref, k_ref, v_ref, o_ref, qb_ref, z_ref, lat_ref, hi_ref, lo_ref, w_ref,
                      carry_ref, rs_ref, acc_ref):
    qi = pl.program_id(2)
    blk = ATT_BLOCK
    n_chunks = KV_GROUP * blk // ROW_CHUNK
    qb_ref[...] = _stack_heads(q_ref[...]).astype(BF16)
    carry_ref[...] = jnp.zeros_like(carry_ref)
    acc_ref[...] = jnp.zeros_like(acc_ref)
    upper = _strict_upper(blk)

    def attend(j, diagonal):
        start = pl.multiple_of(j * blk, blk)
        z_ref[...] = _dot_nt(qb_ref[...], k_ref[pl.ds(start, blk), :].astype(BF16))

        def causal(c):
            q_pos, k_pos = _chunk_positions(c, blk)
            return k_pos < q_pos

        def keep_terms(c, carry):
            r = _row_chunk(c)
            log_beta, log_keep = _log_sigmoid_pair(z_ref[r, :] * ATTN_SCALE)
            if diagonal:
                log_keep = jnp.where(causal(c), log_keep, 0.0)
            hi = log_keep.astype(BF16)
            hi_ref[r, :] = hi
            lo_ref[r, :] = (log_keep - hi.astype(F32)).astype(BF16)
            z_ref[r, :] = log_beta + _lane_tile(carry_ref[r, :], blk)
            rs_ref[r, :] = jnp.broadcast_to(jnp.sum(log_keep, axis=1, keepdims=True), (ROW_CHUNK, LANES))
            return carry

        lax.fori_loop(0, n_chunks, keep_terms, 0, unroll=CHUNK_UNROLL)
        lat_ref[...] = _dot(hi_ref[...], upper) + _dot(lo_ref[...], upper)

        def weights(c, carry):
            r = _row_chunk(c)
            w = jnp.exp(z_ref[r, :] + lat_ref[r, :])
            if diagonal:
                w = jnp.where(causal(c), w, 0.0)
            w_ref[r, :] = w.astype(BF16)
            return carry

        lax.fori_loop(0, n_chunks, weights, 0, unroll=CHUNK_UNROLL)
        acc_ref[...] += _dot(w_ref[...], v_ref[pl.ds(start, blk), :].astype(BF16))
        carry_ref[...] += rs_ref[...]

    attend(qi, True)

    def more(state):
        t, alive = state
        return (t < qi) & (alive == 1)

    def older(state):
        t, _ = state
        attend(qi - 1 - t, False)
        return t + 1, _still_alive(carry_ref)

    lax.while_loop(more, older, (jnp.int32(0), _still_alive(carry_ref)))
    _store_heads(o_ref, acc_ref[...])


def _sb_prompt(q, k, v, batch, seq):
    n_q = seq // ATT_BLOCK
    gw = KV_GROUP * HEAD_DIM
    rows = KV_GROUP * ATT_BLOCK
    kv_spec = pl.BlockSpec((seq, HEAD_DIM), lambda b, g, i: (b, g))
    return pl.pallas_call(
        _sb_prompt_kernel, grid=(batch, N_KV_HEADS, n_q),
        in_specs=[pl.BlockSpec((ATT_BLOCK, gw), lambda b, g, i: (b * n_q + i, g)), kv_spec, kv_spec],
        out_specs=pl.BlockSpec((ATT_BLOCK, gw), lambda b, g, i: (b * n_q + i, g)),
        out_shape=jax.ShapeDtypeStruct((batch * seq, D_MODEL), BF16),
        scratch_shapes=[pltpu.VMEM((rows, HEAD_DIM), BF16),
                        pltpu.VMEM((rows, ATT_BLOCK), F32),
                        pltpu.VMEM((rows, ATT_BLOCK), F32),
                        pltpu.VMEM((rows, ATT_BLOCK), BF16), pltpu.VMEM((rows, ATT_BLOCK), BF16),
                        pltpu.VMEM((rows, ATT_BLOCK), BF16),
                        pltpu.VMEM((rows, LANES), F32), pltpu.VMEM((rows, LANES), F32),
                        pltpu.VMEM((rows, HEAD_DIM), F32)],
        compiler_params=_cparams(("parallel", "parallel", "arbitrary")), name="sb_prompt")(q, k, v)


def _group_rows(q, dec_batch, dec_seq):
    q = q.reshape(dec_batch, dec_seq, N_KV_HEADS, KV_GROUP, HEAD_DIM)
    return q.transpose(0, 2, 1, 3, 4).reshape(dec_batch, N_KV_HEADS, dec_seq * KV_GROUP, HEAD_DIM)


def _ungroup_rows(o, dec_batch, dec_seq):
    o = o.reshape(dec_batch, N_KV_HEADS, dec_seq, KV_GROUP, HEAD_DIM)
    return o.transpose(0, 2, 1, 3, 4).reshape(dec_batch * dec_seq, D_MODEL)


def _pad_new(x, dec_batch, dec_seq):
    x = x.reshape(dec_batch, dec_seq, KV_WIDTH)
    return jnp.pad(x, ((0, 0), (0, PAGE_SIZE - dec_seq), (0, 0)))


def _page_pair(ref0, ref1, g):
    return jnp.concatenate([ref0[:, g, :], ref1[:, g, :]], axis=0).astype(BF16)


def _moba_sample_kernel(pt_ref, q_ref, kn_ref, vn_ref, *refs, n_blk, dec_seq):
    del pt_ref
    pages = PAGES_PER_STEP
    k_refs, v_refs = refs[:pages], refs[pages:2 * pages]
    o_ref, s_ref, p_ref, kmean_ref, acc_ref, l_ref = refs[2 * pages:]
    step = pl.program_id(1)
    rg = dec_seq * KV_GROUP
    bps = pages // 2
    k_steps = n_blk // bps

    @pl.when(step == 0)
    def _():
        kmean_ref[...] = jnp.zeros_like(kmean_ref)

    @pl.when(step < k_steps)
    def _():
        for g in range(N_KV_HEADS):
            qb = q_ref[g].astype(BF16)
            for bl in range(bps):
                j = step * bps + bl
                kf = jnp.concatenate([k_refs[2 * bl][:, g, :], k_refs[2 * bl + 1][:, g, :]], axis=0)
                kmean_ref[g, pl.ds(j, 1), :] = jnp.mean(kf, axis=0, keepdims=True)
                s_ref[j, g * rg:(g + 1) * rg, :] = _dot_nt(qb, kf.astype(BF16)) * ATTN_SCALE

    @pl.when(step == k_steps)
    def _():
        sel_g, s_new_g = [], []
        tok = lax.broadcasted_iota(jnp.int32, (rg, PAGE_SIZE), 0) // KV_GROUP
        key = lax.broadcasted_iota(jnp.int32, (rg, PAGE_SIZE), 1)
        for g in range(N_KV_HEADS):
            gate = _dot_nt(q_ref[g].astype(BF16), kmean_ref[g].astype(BF16))
            sel_g.append(_topk_lane_mask(gate, n_blk, n_blk).astype(F32))
            kn = kn_ref[:, g * HEAD_DIM:(g + 1) * HEAD_DIM].astype(BF16)
            s_new = _dot_nt(q_ref[g].astype(BF16), kn) * ATTN_SCALE
            s_new_g.append(jnp.where(key <= tok, s_new, NEG_BIG))
        sel = jnp.concatenate(sel_g, axis=0)
        s_new = jnp.concatenate(s_new_g, axis=0)
        m = jnp.max(s_new, axis=1, keepdims=True)
        for j in range(n_blk):
            picked = sel[:, j:j + 1] > 0.5
            m = jnp.maximum(m, jnp.max(jnp.where(picked, s_ref[j], NEG_BIG), axis=1, keepdims=True))
        p_new = jnp.exp(s_new - m)
        l = jnp.sum(p_new, axis=1, keepdims=True)
        for j in range(n_blk):
            picked = sel[:, j:j + 1] > 0.5
            p = jnp.where(picked, jnp.exp(s_ref[j] - m), 0.0)
            l = l + jnp.sum(p, axis=1, keepdims=True)
            p_ref[j] = p.astype(BF16)
        l_ref[...] = l
        for g in range(N_KV_HEADS):
            vn = vn_ref[:, g * HEAD_DIM:(g + 1) * HEAD_DIM].astype(BF16)
            acc_ref[g * rg:(g + 1) * rg, :] = _dot(p_new[g * rg:(g + 1) * rg].astype(BF16), vn)

    @pl.when(step >= k_steps)
    def _():
        for g in range(N_KV_HEADS):
            part = None
            for bl in range(bps):
                j = (step - k_steps) * bps + bl
                t = _dot(p_ref[j, g * rg:(g + 1) * rg, :], _page_pair(v_refs[2 * bl], v_refs[2 * bl + 1], g))
                part = t if part is None else part + t
            acc_ref[g * rg:(g + 1) * rg, :] += part

    @pl.when(step == 2 * k_steps - 1)
    def _():
        out = acc_ref[...] / l_ref[...]
        for g in range(N_KV_HEADS):
            o_ref[g] = out[g * rg:(g + 1) * rg]


def _moba_sample(qg, k_new, v_new, cache_k, cache_v, page_table, dec_seq):
    dec_batch, n_pages = page_table.shape
    n_blk = n_pages // 2
    rg = dec_seq * KV_GROUP
    rows = N_KV_HEADS * rg
    pages = PAGES_PER_STEP
    assert n_pages % pages == 0
    k_steps = n_pages // pages
    page = (None, PAGE_SIZE, N_KV_HEADS, HEAD_DIM)

    def k_map(off):
        return lambda b, s, pt: (pt[b, pages * jnp.minimum(s, k_steps - 1) + off], 0, 0, 0)

    def v_map(off):
        return lambda b, s, pt: (pt[b, pages * jnp.maximum(s - k_steps, 0) + off], 0, 0, 0)

    per_b4 = pl.BlockSpec((None, N_KV_HEADS, rg, HEAD_DIM), lambda b, s, pt: (b, 0, 0, 0))
    per_b3 = pl.BlockSpec((None, PAGE_SIZE, KV_WIDTH), lambda b, s, pt: (b, 0, 0))
    return pl.pallas_call(
        functools.partial(_moba_sample_kernel, n_blk=n_blk, dec_seq=dec_seq),
        grid_spec=pltpu.PrefetchScalarGridSpec(
            num_scalar_prefetch=1, grid=(dec_batch, 2 * k_steps),
            in_specs=[per_b4, per_b3, per_b3]
            + [pl.BlockSpec(page, k_map(i)) for i in range(pages)]
            + [pl.BlockSpec(page, v_map(i)) for i in range(pages)],
            out_specs=per_b4,
            scratch_shapes=[pltpu.VMEM((n_blk, rows, 2 * PAGE_SIZE), F32),
                            pltpu.VMEM((n_blk, rows, 2 * PAGE_SIZE), BF16),
                            pltpu.VMEM((N_KV_HEADS, LANES, HEAD_DIM), F32),
                            pltpu.VMEM((rows, HEAD_DIM), F32),
                            pltpu.VMEM((rows, 1), F32)]),
        out_shape=jax.ShapeDtypeStruct((dec_batch, N_KV_HEADS, rg, HEAD_DIM), F32),
        compiler_params=_cparams(("parallel", "arbitrary")), name="moba_sample",
    )(page_table, qg, k_new, v_new, *([cache_k] * pages), *([cache_v] * pages))


def _sb_sample_kernel(pt_ref, q_ref, kn_ref, vn_ref, *refs, dec_seq):
    del pt_ref
    pages = PAGES_PER_STEP
    k_refs, v_refs = refs[:pages], refs[pages:2 * pages]
    o_ref, acc_ref, carry_ref, alive_ref = refs[2 * pages:]
    step = pl.program_id(1)
    rg = dec_seq * KV_GROUP
    blk = 2 * PAGE_SIZE

    def accumulate(g, kb, vb, upper, causal):
        rows = slice(g * rg, (g + 1) * rg)
        z = _dot_nt(q_ref[g].astype(BF16), kb) * ATTN_SCALE
        log_beta, log_keep = _log_sigmoid_pair(z)
        if causal is not None:
            log_keep = jnp.where(causal, log_keep, 0.0)
        w = jnp.exp(log_beta + _suffix_sums(log_keep, upper) + carry_ref[rows, :])
        if causal is not None:
            w = jnp.where(causal, w, 0.0)
        acc_ref[rows, :] += _dot(w.astype(BF16), vb)
        carry_ref[rows, :] += jnp.sum(log_keep, axis=1, keepdims=True)

    @pl.when(step == 0)
    def _():
        acc_ref[...] = jnp.zeros_like(acc_ref)
        carry_ref[...] = jnp.zeros_like(carry_ref)
        tok = lax.broadcasted_iota(jnp.int32, (rg, PAGE_SIZE), 0) // KV_GROUP
        key = lax.broadcasted_iota(jnp.int32, (rg, PAGE_SIZE), 1)
        upper = _strict_upper(PAGE_SIZE)
        for g in range(N_KV_HEADS):
            cols = slice(g * HEAD_DIM, (g + 1) * HEAD_DIM)
            accumulate(g, kn_ref[:, cols].astype(BF16), vn_ref[:, cols].astype(BF16), upper, key < tok)
        alive_ref[0] = _still_alive(carry_ref)

    @pl.when(alive_ref[0] == 1)
    def _():
        upper = _strict_upper(blk)
        for bl in reversed(range(pages // 2)):
            for g in range(N_KV_HEADS):
                accumulate(g, _page_pair(k_refs[2 * bl], k_refs[2 * bl + 1], g),
                           _page_pair(v_refs[2 * bl], v_refs[2 * bl + 1], g), upper, None)
        alive_ref[0] = _still_alive(carry_ref)

    @pl.when(step == pl.num_programs(1) - 1)
    def _():
        for g in range(N_KV_HEADS):
            o_ref[g] = acc_ref[g * rg:(g + 1) * rg, :]


def _sb_sample(qg, k_new, v_new, cache_k, cache_v, page_table, dec_seq):
    dec_batch, n_pages = page_table.shape
    rg = dec_seq * KV_GROUP
    rows = N_KV_HEADS * rg
    pages = PAGES_PER_STEP
    assert n_pages % pages == 0
    page = (None, PAGE_SIZE, N_KV_HEADS, HEAD_DIM)

    def page_map(off):
        return lambda b, s, pt: (pt[b, n_pages - pages * (s + 1) + off], 0, 0, 0)

    per_b4 = pl.BlockSpec((None, N_KV_HEADS, rg, HEAD_DIM), lambda b, s, pt: (b, 0, 0, 0))
    per_b3 = pl.BlockSpec((None, PAGE_SIZE, KV_WIDTH), lambda b, s, pt: (b, 0, 0))
    page_specs = [pl.BlockSpec(page, page_map(i)) for i in range(pages)]
    return pl.pallas_call(
        functools.partial(_sb_sample_kernel, dec_seq=dec_seq),
        grid_spec=pltpu.PrefetchScalarGridSpec(
            num_scalar_prefetch=1, grid=(dec_batch, n_pages // pages),
            in_specs=[per_b4, per_b3, per_b3] + page_specs + page_specs,
            out_specs=per_b4,
            scratch_shapes=[pltpu.VMEM((rows, HEAD_DIM), F32), pltpu.VMEM((rows, 1), F32),
                            pltpu.SMEM((1,), jnp.int32)]),
        out_shape=jax.ShapeDtypeStruct((dec_batch, N_KV_HEADS, rg, HEAD_DIM), F32),
        compiler_params=_cparams(("parallel", "arbitrary")), name="sb_sample",
    )(page_table, qg, k_new, v_new, *([cache_k] * pages), *([cache_v] * pages))


def _pool_kernel(x_ref, w_ref, scale_ref, o_ref, *, seq, chunk, counts):
    g = pl.program_id(1)
    halo = POOL_STATE + 1
    w = w_ref[...].astype(BF16)
    for c0 in range(0, seq, chunk):
        cur = x_ref[halo + c0:halo + c0 + chunk, :]
        tot, out = cur, None
        for d in range(1, max(POOL_WINDOWS)):
            tot = tot + x_ref[halo + c0 - d:halo + c0 - d + chunk, :]
            if d + 1 in POOL_WINDOWS:
                gi = POOL_WINDOWS.index(d + 1)
                mixed = tot / counts(gi, c0, chunk) - cur
                out = mixed if out is None else jnp.where(g == gi, mixed, out)
        o_ref[c0:c0 + chunk, :] = _dot(out.astype(BF16), w) * scale_ref[...]


def _pool_mixer(x_hist, w_pool, scale, seq, first_pos):
    batch = x_hist.shape[0]
    halo = POOL_STATE + 1
    chunk = min(seq, 512)

    def counts(gi, c0, rows):
        win = POOL_WINDOWS[gi]
        pos = first_pos + c0 + lax.broadcasted_iota(jnp.int32, (rows, 1), 0)
        return jnp.minimum(win, pos + 1).astype(F32)

    return pl.pallas_call(
        functools.partial(_pool_kernel, seq=seq, chunk=chunk, counts=counts),
        grid=(batch, len(POOL_WINDOWS)),
        in_specs=[pl.BlockSpec((None, halo + seq, POOL_GROUP), lambda b, g: (b, 0, g)),
                  pl.BlockSpec((None, POOL_GROUP, POOL_GROUP), lambda b, g: (g, 0, 0)),
                  pl.BlockSpec((1, POOL_GROUP), lambda b, g: (0, g))],
        out_specs=pl.BlockSpec((None, seq, POOL_GROUP), lambda b, g: (b, 0, g)),
        out_shape=jax.ShapeDtypeStruct((batch, seq, D_MODEL), F32),
        compiler_params=_cparams(("parallel", "parallel")), name="pool_mixer",
    )(x_hist, w_pool, scale.reshape(1, D_MODEL))


def _rope_tables(pos):
    half = ROT_DIM // 2
    inv = ROPE_THETA ** (-jnp.arange(0, ROT_DIM, 2, dtype=F32) / ROT_DIM)
    ang = pos.astype(F32)[:, None] * inv[None, :]
    cos, sin = jnp.cos(ang), jnp.sin(ang)
    n = pos.shape[0]
    ones = jnp.ones((n, HEAD_DIM - ROT_DIM), F32)
    zeros = jnp.zeros((n, HEAD_DIM - half), F32)
    a = jnp.concatenate([cos, cos, ones], axis=1)
    b = jnp.concatenate([-sin, zeros], axis=1)
    c = jnp.concatenate([jnp.zeros((n, half), F32), sin, jnp.zeros((n, HEAD_DIM - ROT_DIM), F32)], axis=1)
    return a, b, c


def kernel(x_prompt, x_sample, cache_k_l0, cache_v_l0, state_pool_l1, cache_k_l2, cache_v_l2, cache_k_l3, cache_v_l3, page_table, l0_wq, l0_wk, l0_wv, l0_wo, l0_ln1_g, l0_ln1_b, l0_ffn_wg, l0_ffn_wu, l0_ffn_wd, l0_ln2_g, l0_ln2_b, l1_pool_w, l1_pool_scale, l1_ln1_g, l1_ln1_b, l1_router, l1_moe_wg, l1_moe_wu, l1_moe_wd, l1_ln2_g, l1_ln2_b, l2_wq, l2_wk, l2_wv, l2_wo, l2_ln1_g, l2_ln1_b, l2_ffn_wg, l2_ffn_wu, l2_ffn_wd, l2_ln2_g, l2_ln2_b, l3_wq, l3_wk, l3_wv, l3_wo, l3_ln1_g, l3_ln1_b, l3_router, l3_moe_wg, l3_moe_wu, l3_moe_wd, l3_ln2_g, l3_ln2_b):
    batch, seq, _ = x_prompt.shape
    dec_batch, dec_seq, _ = x_sample.shape
    n_prompt = batch * seq
    n_sample = dec_batch * dec_seq
    n_tokens = n_prompt + n_sample
    m_pad = -(-n_tokens // ROW_TILE) * ROW_TILE
    assert m_pad % LN_ROW_TILE == 0 and seq % ATT_BLOCK == 0 and PAST_LEN % MOBA_BLOCK == 0
    assert PAST_LEN // MOBA_BLOCK == (PAST_LEN + dec_seq - 1) // MOBA_BLOCK and dec_seq <= PAGE_SIZE

    x = jnp.concatenate([x_prompt.reshape(n_prompt, D_MODEL), x_sample.reshape(n_sample, D_MODEL),
                         jnp.zeros((m_pad - n_tokens, D_MODEL), F32)], axis=0)
    xb = x.astype(BF16)
    pos = jnp.concatenate([jnp.tile(jnp.arange(seq, dtype=jnp.int32), batch),
                           jnp.tile(PAST_LEN + jnp.arange(dec_seq, dtype=jnp.int32), dec_batch),
                           jnp.zeros((m_pad - n_tokens,), jnp.int32)])
    rope = _rope_tables(pos)

    def split(y, width):
        p = y[:n_prompt].reshape(batch, seq, width)
        s = y[n_prompt:n_tokens].reshape(dec_batch, dec_seq, width)
        return p, s

    def attention_layer(kind, x, xb, w, caches):
        wq, wk, wv, wo = w
        q = _project(xb, wq, rope if kind == 0 else None)
        k = _project(xb, wk, rope if kind == 0 else None)
        v = _project(xb, wv)
        prompt_fn = _moba_prompt if kind == 0 else _sb_prompt
        sample_fn = _moba_sample if kind == 0 else _sb_sample
        o_p = prompt_fn(q, k, v, batch, seq)
        qg = _group_rows(q[n_prompt:n_tokens], dec_batch, dec_seq)
        o_s = sample_fn(qg, _pad_new(k[n_prompt:n_tokens], dec_batch, dec_seq),
                        _pad_new(v[n_prompt:n_tokens], dec_batch, dec_seq), caches[0], caches[1], page_table, dec_seq)
        o_s = _ungroup_rows(o_s, dec_batch, dec_seq).astype(BF16)
        o = jnp.concatenate([o_p, o_s, jnp.zeros((m_pad - n_tokens, D_MODEL), BF16)], axis=0)
        kp, ks = split(k, KV_WIDTH)
        vp, vs = split(v, KV_WIDTH)
        shape4 = lambda t: t.reshape(t.shape[0], t.shape[1], N_KV_HEADS, HEAD_DIM)
        return o, wo, (shape4(kp), shape4(vp), shape4(ks), shape4(vs))

    def pool_layer(x):
        xp, xs = split(x, D_MODEL)
        hist_p = jnp.concatenate([jnp.zeros((batch, POOL_STATE + 1, D_MODEL), F32), xp], axis=1)
        seq_s = -(-dec_seq // SUBLANES) * SUBLANES
        hist_s = jnp.concatenate([jnp.zeros((dec_batch, 1, D_MODEL), F32), state_pool_l1, xs,
                                  jnp.zeros((dec_batch, seq_s - dec_seq, D_MODEL), F32)], axis=1)
        h_p = _pool_mixer(hist_p, l1_pool_w, l1_pool_scale, seq, 0)
        h_s = _pool_mixer(hist_s, l1_pool_w, l1_pool_scale, seq_s, PAST_LEN)[:, :dec_seq]
        h = jnp.concatenate([h_p.reshape(n_prompt, D_MODEL), h_s.reshape(n_sample, D_MODEL),
                             jnp.zeros((m_pad - n_tokens, D_MODEL), F32)], axis=0)
        state_p = hist_p[:, -POOL_STATE:]
        state_s = jnp.concatenate([state_pool_l1, xs], axis=1)[:, -POOL_STATE:]
        return h, (state_p, state_s)

    def dense_ffn(x, xb, w, g, b):
        wg, wu, wd = w
        return _down_residual_ln(_gate_up(xb, wg, wu), wd, x, g, b)

    o, wo, kv0 = attention_layer(0, x, xb, (l0_wq, l0_wk, l0_wv, l0_wo), (cache_k_l0, cache_v_l0))
    x, xb = _down_residual_ln(o, wo, x, l0_ln1_g, l0_ln1_b)
    x, xb = dense_ffn(x, xb, (l0_ffn_wg, l0_ffn_wu, l0_ffn_wd), l0_ln2_g, l0_ln2_b)
    h, pool_state = pool_layer(x)
    x, xb = _residual_ln(x, h, l1_ln1_g, l1_ln1_b)
    x, xb = _moe_block(x, xb, n_tokens, l1_router, l1_moe_wg, l1_moe_wu, l1_moe_wd, l1_ln2_g, l1_ln2_b)
    o, wo, kv2 = attention_layer(2, x, xb, (l2_wq, l2_wk, l2_wv, l2_wo), (cache_k_l2, cache_v_l2))
    x, xb = _down_residual_ln(o, wo, x, l2_ln1_g, l2_ln1_b)
    x, xb = dense_ffn(x, xb, (l2_ffn_wg, l2_ffn_wu, l2_ffn_wd), l2_ln2_g, l2_ln2_b)
    o, wo, kv3 = attention_layer(0, x, xb, (l3_wq, l3_wk, l3_wv, l3_wo), (cache_k_l3, cache_v_l3))
    x, xb = _down_residual_ln(o, wo, x, l3_ln1_g, l3_ln1_b)
    x, xb = _moe_block(x, xb, n_tokens, l3_router, l3_moe_wg, l3_moe_wu, l3_moe_wd, l3_ln2_g, l3_ln2_b)

    y_prompt, y_sample = split(x, D_MODEL)
    return (y_prompt, y_sample, kv0[0], kv0[1], kv0[2], kv0[3], pool_state[0], pool_state[1],
            kv2[0], kv2[1], kv2[2], kv2[3], kv3[0], kv3[1], kv3[2], kv3[3])
```

```python
import functools
import math

import jax
import jax.numpy as jnp
from jax import lax
from jax.experimental import pallas as pl
from jax.experimental.pallas import tpu as pltpu

F32 = jnp.float32
BF16 = jnp.bfloat16

D_MODEL = 2048
DEPTH = 4
PAST_LEN = 16384
PAGE_SIZE = 128
N_HEADS = 16
N_KV_HEADS = 4
HEAD_DIM = D_MODEL // N_HEADS
KV_GROUP = N_HEADS // N_KV_HEADS
KV_WIDTH = N_KV_HEADS * HEAD_DIM
ROT_DIM = HEAD_DIM // 4
ROPE_THETA = 500000.0
MOBA_BLOCK = 256
MOBA_TOPK = 3
POOL_WINDOWS = (2, 4, 8, 16)
POOL_GROUP = D_MODEL // len(POOL_WINDOWS)
POOL_STATE = max(POOL_WINDOWS) - 1
N_EXPERTS = 8
TOP_K = 2
ALPHA = (2 * DEPTH) ** 0.25
LN_EPS = 1e-5
ATTN_SCALE = HEAD_DIM ** -0.5

LANES = 128
SUBLANES = 8
VMEM_LIMIT_BYTES = 56 * 1024 * 1024

ROW_TILE = 1056
LN_ROW_TILE = 528
FF_TILE = 512
GROUP_TILE = 512
SUPER_ROWS = 2048
MOE_FF_TILE = 512
MOE_DOWN_K_TILE = 1024
MOE_DOWN_N_TILE = 1024
COMBINE_TILE = 264
DOWN_ROW_TILE = 1056
DOWN_K_TILE = 512
LN_CHUNK = 96
ROW_CHUNK = 64
CHUNK_UNROLL = 16
PAGES_PER_STEP = 16
ATT_BLOCK = 256
NEG_BIG = -1e30
EXP_ZERO_BELOW = -104.0


def _cparams(sem):
    return pltpu.CompilerParams(dimension_semantics=sem, vmem_limit_bytes=VMEM_LIMIT_BYTES)


def _dot(a, b):
    return jnp.dot(a, b, preferred_element_type=F32)


def _dot_nt(a, b):
    return lax.dot_general(a, b, (((1,), (1,)), ((), ())), preferred_element_type=F32)


def _layer_norm_rows(y, g, b):
    mu = jnp.mean(y, axis=-1, keepdims=True)
    d = y - mu
    var = jnp.mean(d * d, axis=-1, keepdims=True)
    return d * lax.rsqrt(var + LN_EPS) * g + b


def _proj_kernel(x_ref, w_ref, o_ref):
    o_ref[...] = _dot(x_ref[...].astype(BF16), w_ref[...].astype(BF16))


def _proj_rope_kernel(x_ref, w_ref, ca_ref, sb_ref, sc_ref, o_ref):
    y = _dot(x_ref[...].astype(BF16), w_ref[...].astype(BF16))
    half = ROT_DIM // 2
    ca, sb, sc = ca_ref[...], sb_ref[...], sc_ref[...]
    for h in range(y.shape[1] // HEAD_DIM):
        yh = y[:, h * HEAD_DIM:(h + 1) * HEAD_DIM]
        up = pltpu.roll(yh, HEAD_DIM - half, axis=1)
        down = pltpu.roll(yh, half, axis=1)
        o_ref[:, h * HEAD_DIM:(h + 1) * HEAD_DIM] = yh * ca + up * sb + down * sc


def _project(x, w, rope=None, col_tile=512):
    m, kdim = x.shape
    n = w.shape[1]
    tn = min(col_tile, n)
    grid = (m // ROW_TILE, n // tn)
    x_spec = pl.BlockSpec((ROW_TILE, kdim), lambda i, j: (i, 0))
    w_spec = pl.BlockSpec((kdim, tn), lambda i, j: (0, j))
    o_spec = pl.BlockSpec((ROW_TILE, tn), lambda i, j: (i, j))
    if rope is None:
        return pl.pallas_call(
            _proj_kernel, grid=grid, in_specs=[x_spec, w_spec], out_specs=o_spec,
            out_shape=jax.ShapeDtypeStruct((m, n), F32),
            compiler_params=_cparams(("parallel", "arbitrary")), name="project")(x, w)
    t_spec = pl.BlockSpec((ROW_TILE, HEAD_DIM), lambda i, j: (i, 0))
    return pl.pallas_call(
        _proj_rope_kernel, grid=grid, in_specs=[x_spec, w_spec, t_spec, t_spec, t_spec],
        out_specs=o_spec, out_shape=jax.ShapeDtypeStruct((m, n), F32),
        compiler_params=_cparams(("parallel", "arbitrary")), name="project_rope")(x, w, *rope)


def _gateup_kernel(x_ref, wg_ref, wu_ref, h_ref):
    x = x_ref[...].astype(BF16)
    a = _dot(x, wg_ref[...].astype(BF16))
    u = _dot(x, wu_ref[...].astype(BF16))
    h_ref[...] = (a * (1.0 / (1.0 + jnp.exp(-a))) * u).astype(h_ref.dtype)


def _gate_up(x, wg, wu):
    m, kdim = x.shape
    n = wg.shape[1]
    grid = (m // ROW_TILE, n // FF_TILE)
    return pl.pallas_call(
        _gateup_kernel, grid=grid,
        in_specs=[pl.BlockSpec((ROW_TILE, kdim), lambda i, j: (i, 0)),
                  pl.BlockSpec((kdim, FF_TILE), lambda i, j: (0, j)),
                  pl.BlockSpec((kdim, FF_TILE), lambda i, j: (0, j))],
        out_specs=pl.BlockSpec((ROW_TILE, FF_TILE), lambda i, j: (i, j)),
        out_shape=jax.ShapeDtypeStruct((m, n), BF16),
        compiler_params=_cparams(("parallel", "arbitrary")), name="gate_up")(x, wg, wu)


def _down_ln_kernel(h_ref, w_ref, res_ref, g_ref, b_ref, y_ref, yb_ref, acc_ref):
    k = pl.program_id(1)

    @pl.when(k == 0)
    def _():
        acc_ref[...] = jnp.zeros_like(acc_ref)

    acc_ref[...] += _dot(h_ref[...].astype(BF16), w_ref[...].astype(BF16))

    @pl.when(k == pl.num_programs(1) - 1)
    def _():
        rows = acc_ref.shape[0]
        chunk = math.gcd(rows, LN_CHUNK)

        def body(c, carry):
            r = pl.ds(pl.multiple_of(c * chunk, 2 * SUBLANES), chunk)
            y = _layer_norm_rows(ALPHA * res_ref[r, :] + acc_ref[r, :], g_ref[...], b_ref[...])
            y_ref[r, :] = y
            yb_ref[r, :] = y.astype(BF16)
            return carry

        lax.fori_loop(0, rows // chunk, body, 0)


def _down_residual_ln(h, w, res, g, b):
    m, kdim = h.shape
    n = w.shape[1]
    tm = DOWN_ROW_TILE
    tk = min(DOWN_K_TILE, kdim)
    grid = (m // tm, kdim // tk)
    row = lambda i, k: (i, 0)
    return pl.pallas_call(
        _down_ln_kernel, grid=grid,
        in_specs=[pl.BlockSpec((tm, tk), lambda i, k: (i, k)),
                  pl.BlockSpec((tk, n), lambda i, k: (k, 0)),
                  pl.BlockSpec((tm, n), row, pipeline_mode=pl.Buffered(1)),
                  pl.BlockSpec((1, n), lambda i, k: (0, 0)),
                  pl.BlockSpec((1, n), lambda i, k: (0, 0))],
        out_specs=[pl.BlockSpec((tm, n), row), pl.BlockSpec((tm, n), row)],
        out_shape=[jax.ShapeDtypeStruct((m, n), F32), jax.ShapeDtypeStruct((m, n), BF16)],
        scratch_shapes=[pltpu.VMEM((tm, n), F32)],
        compiler_params=_cparams(("parallel", "arbitrary")), name="down_residual_ln",
    )(h, w, res, g.reshape(1, n), b.reshape(1, n))


def _residual_ln_kernel(res_ref, h_ref, g_ref, b_ref, y_ref, yb_ref):
    y = _layer_norm_rows(ALPHA * res_ref[...] + h_ref[...], g_ref[...], b_ref[...])
    y_ref[...] = y
    yb_ref[...] = y.astype(BF16)


def _residual_ln(res, h, g, b):
    m, n = res.shape
    row = pl.BlockSpec((LN_ROW_TILE, n), lambda i: (i, 0))
    vec = pl.BlockSpec((1, n), lambda i: (0, 0))
    return pl.pallas_call(
        _residual_ln_kernel, grid=(m // LN_ROW_TILE,), in_specs=[row, row, vec, vec],
        out_specs=[row, row],
        out_shape=[jax.ShapeDtypeStruct((m, n), F32), jax.ShapeDtypeStruct((m, n), BF16)],
        compiler_params=_cparams(("parallel",)), name="residual_ln",
    )(res, h, g.reshape(1, n), b.reshape(1, n))


def _router_kernel(x_ref, w_ref, idx_ref, gate_ref):
    logits = _dot(x_ref[...].astype(BF16), w_ref[...].astype(BF16))
    lane = lax.broadcasted_iota(jnp.int32, logits.shape, 1)
    logits = jnp.where(lane < N_EXPERTS, logits, -jnp.inf)
    v1 = jnp.max(logits, axis=1, keepdims=True)
    i1 = jnp.min(jnp.where(logits == v1, lane, LANES), axis=1, keepdims=True)
    rest = jnp.where(lane == i1, -jnp.inf, logits)
    v2 = jnp.max(rest, axis=1, keepdims=True)
    i2 = jnp.min(jnp.where(rest == v2, lane, LANES), axis=1, keepdims=True)
    e = jnp.exp(v2 - v1)
    g1 = 1.0 / (1.0 + e)
    g2 = e / (1.0 + e)
    idx_ref[...] = jnp.where(lane == 0, i1, jnp.where(lane == 1, i2, 0))
    gate_ref[...] = jnp.where(lane == 0, g1, jnp.where(lane == 1, g2, 0.0))


def _route(x, w_router):
    m, kdim = x.shape
    w_pad = jnp.zeros((kdim, LANES), F32).at[:, :N_EXPERTS].set(w_router)
    out = pl.BlockSpec((LN_ROW_TILE, LANES), lambda i: (i, 0))
    return pl.pallas_call(
        _router_kernel, grid=(m // LN_ROW_TILE,),
        in_specs=[pl.BlockSpec((LN_ROW_TILE, kdim), lambda i: (i, 0)),
                  pl.BlockSpec((kdim, LANES), lambda i: (0, 0))],
        out_specs=[out, out],
        out_shape=[jax.ShapeDtypeStruct((m, LANES), jnp.int32), jax.ShapeDtypeStruct((m, LANES), F32)],
        compiler_params=_cparams(("parallel",)), name="router")(x, w_pad)


def _routing_plan(top_idx, n_sup):
    tiles_per_sup = SUPER_ROWS // GROUP_TILE
    e_flat = top_idx.reshape(-1)
    onehot = (e_flat[:, None] == jnp.arange(N_EXPERTS, dtype=jnp.int32)[None, :]).astype(jnp.int32)
    counts = jnp.sum(onehot, axis=0)
    rank = jnp.sum((jnp.cumsum(onehot, axis=0) - onehot) * onehot, axis=1)
    sups = (counts + SUPER_ROWS - 1) // SUPER_ROWS
    sup_end = jnp.cumsum(sups)
    sup_start = sup_end - sups
    dest = ((sup_start * SUPER_ROWS)[e_flat] + rank).astype(jnp.int32)
    s = jnp.arange(n_sup, dtype=jnp.int32)
    used = sup_end[-1]
    s_eff = jnp.minimum(s, used - 1).astype(jnp.int32)
    sup_expert = jnp.sum((s_eff[:, None] >= sup_end[None, :]).astype(jnp.int32), axis=1)
    rows_left = counts[sup_expert] - (s_eff - sup_start[sup_expert]) * SUPER_ROWS
    sup_tiles = jnp.clip((rows_left + GROUP_TILE - 1) // GROUP_TILE, 0, tiles_per_sup) * (s < used)
    src = jnp.zeros((n_sup * SUPER_ROWS,), jnp.int32).at[dest].set(
        jnp.arange(e_flat.shape[0], dtype=jnp.int32) // TOP_K, unique_indices=True)
    return dest, sup_expert.astype(jnp.int32), s_eff, sup_tiles.astype(jnp.int32), src


def _gather_kernel(st_ref, src_ref, x_hbm, o_ref, buf, sem):
    t = pl.program_id(0)
    rows = buf.shape[0]
    tiles_per_sup = SUPER_ROWS // rows
    populated = lax.rem(t, tiles_per_sup) < st_ref[lax.div(t, tiles_per_sup)]

    def copy(r):
        return pltpu.make_async_copy(x_hbm.at[pl.ds(src_ref[0, r], 1)], buf.at[pl.ds(r, 1)], sem)

    def start(r, c):
        copy(r).start()
        return c

    def wait(r, c):
        copy(r).wait()
        return c

    @pl.when(populated)
    def _():
        lax.fori_loop(0, rows, start, 0, unroll=8)
        lax.fori_loop(0, rows, wait, 0, unroll=8)
        o_ref[...] = buf[...].astype(o_ref.dtype)

    @pl.when(jnp.logical_not(populated))
    def _():
        o_ref[...] = jnp.zeros_like(o_ref)


def _gather_rows(x, plan, n_sup):
    n = x.shape[1]
    n_tiles = n_sup * (SUPER_ROWS // GROUP_TILE)
    src = plan[4].reshape(n_tiles, 1, GROUP_TILE)
    return pl.pallas_call(
        _gather_kernel,
        grid_spec=pltpu.PrefetchScalarGridSpec(
            num_scalar_prefetch=1, grid=(n_tiles,),
            in_specs=[pl.BlockSpec((None, 1, GROUP_TILE), lambda t, st: (t, 0, 0), memory_space=pltpu.SMEM),
                      pl.BlockSpec(memory_space=pl.ANY)],
            out_specs=pl.BlockSpec((GROUP_TILE, n), lambda t, st: (t, 0)),
            scratch_shapes=[pltpu.VMEM((GROUP_TILE, n), x.dtype), pltpu.SemaphoreType.DMA(())]),
        out_shape=jax.ShapeDtypeStruct((n_tiles * GROUP_TILE, n), BF16),
        compiler_params=_cparams(("arbitrary",)), name="moe_gather")(plan[3], src, x)


def _moe_gateup_kernel(se_ref, ss_ref, st_ref, x_ref, wg_ref, wu_ref, h_ref):
    del se_ref, ss_ref
    tiles = st_ref[pl.program_id(0)]

    for n in range(1, SUPER_ROWS // GROUP_TILE + 1):
        @pl.when(tiles == n)
        def _(rows=n * GROUP_TILE):
            x = x_ref[:rows, :]
            a = _dot(x, wg_ref[...].astype(BF16))
            u = _dot(x, wu_ref[...].astype(BF16))
            h_ref[:rows, :] = (a * (1.0 / (1.0 + jnp.exp(-a))) * u).astype(h_ref.dtype)
            if rows < SUPER_ROWS:
                h_ref[rows:, :] = jnp.zeros((SUPER_ROWS - rows, h_ref.shape[1]), h_ref.dtype)

    @pl.when(tiles == 0)
    def _():
        h_ref[...] = jnp.zeros_like(h_ref)


def _moe_gate_up(xs, wg, wu, plan, n_sup):
    _, sup_expert, sup_src, sup_tiles, _ = plan
    kdim = xs.shape[1]
    n = wg.shape[2]
    tn = min(MOE_FF_TILE, n)
    n_j = n // tn

    def w_map(s, j, se, ss, st):
        return se[s], 0, jnp.where(st[s] > 0, j, n_j - 1)

    w_spec = pl.BlockSpec((None, kdim, tn), w_map)
    return pl.pallas_call(
        _moe_gateup_kernel,
        grid_spec=pltpu.PrefetchScalarGridSpec(
            num_scalar_prefetch=3, grid=(n_sup, n_j),
            in_specs=[pl.BlockSpec((SUPER_ROWS, kdim), lambda s, j, se, ss, st: (ss[s], 0)), w_spec, w_spec],
            out_specs=pl.BlockSpec((SUPER_ROWS, tn), lambda s, j, se, ss, st: (s, j))),
        out_shape=jax.ShapeDtypeStruct((n_sup * SUPER_ROWS, n), BF16),
        compiler_params=_cparams(("arbitrary", "arbitrary")), name="moe_gate_up",
    )(sup_expert, sup_src, sup_tiles, xs, wg, wu)


def _moe_down_kernel(se_ref, ss_ref, st_ref, h_ref, w_ref, y_ref):
    del se_ref, ss_ref
    tiles = st_ref[pl.program_id(0)]
    k = pl.program_id(2)

    @pl.when(k == 0)
    def _():
        y_ref[...] = jnp.zeros_like(y_ref)

    for n in range(1, SUPER_ROWS // GROUP_TILE + 1):
        @pl.when(tiles == n)
        def _(rows=n * GROUP_TILE):
            y_ref[:rows, :] += _dot(h_ref[:rows, :], w_ref[...].astype(BF16))


def _moe_down(h, wd, plan, n_sup):
    _, sup_expert, sup_src, sup_tiles, _ = plan
    kdim = h.shape[1]
    n = wd.shape[2]
    tk = min(MOE_DOWN_K_TILE, kdim)
    tn = min(MOE_DOWN_N_TILE, n)
    n_k = kdim // tk

    def h_map(s, c, k, se, ss, st):
        return ss[s], jnp.where(st[s] > 0, k, n_k - 1)

    def w_map(s, c, k, se, ss, st):
        return se[s], jnp.where(st[s] > 0, k, n_k - 1), c

    return pl.pallas_call(
        _moe_down_kernel,
        grid_spec=pltpu.PrefetchScalarGridSpec(
            num_scalar_prefetch=3, grid=(n_sup, n // tn, n_k),
            in_specs=[pl.BlockSpec((SUPER_ROWS, tk), h_map), pl.BlockSpec((None, tk, tn), w_map)],
            out_specs=pl.BlockSpec((SUPER_ROWS, tn), lambda s, c, k, se, ss, st: (s, c))),
        out_shape=jax.ShapeDtypeStruct((n_sup * SUPER_ROWS, n), F32),
        compiler_params=_cparams(("arbitrary", "arbitrary", "arbitrary")), name="moe_down",
    )(sup_expert, sup_src, sup_tiles, h, wd)


def _combine_ln_kernel(dest_ref, ys_hbm, res_ref, gate_ref, g_ref, b_ref, y_ref, yb_ref, buf, sem,
                       *, tokens_per_step, n_tokens):
    base = pl.program_id(0) * tokens_per_step

    def copy(t, s):
        tok = jnp.minimum(base + t, n_tokens - 1)
        return pltpu.make_async_copy(
            ys_hbm.at[pl.ds(dest_ref[tok * TOP_K + s], 1)], buf.at[s, pl.ds(t, 1)], sem)

    def start(t, c):
        for s in range(TOP_K):
            copy(t, s).start()
        return c

    def wait(t, c):
        for s in range(TOP_K):
            copy(t, s).wait()
        return c

    lax.fori_loop(0, tokens_per_step, start, 0, unroll=4)
    lax.fori_loop(0, tokens_per_step, wait, 0, unroll=4)
    gate = gate_ref[...]
    f = gate[:, 0:1] * buf[0] + gate[:, 1:2] * buf[1]
    y = _layer_norm_rows(ALPHA * res_ref[...] + f, g_ref[...], b_ref[...])
    y_ref[...] = y
    yb_ref[...] = y.astype(BF16)


def _combine_residual_ln(ys, dest, gates, res, g, b, n_tokens):
    m, n = res.shape
    tps = COMBINE_TILE
    row = lambda i, d: (i, 0)
    vec = pl.BlockSpec((1, n), lambda i, d: (0, 0))
    return pl.pallas_call(
        functools.partial(_combine_ln_kernel, tokens_per_step=tps, n_tokens=n_tokens),
        grid_spec=pltpu.PrefetchScalarGridSpec(
            num_scalar_prefetch=1, grid=(m // tps,),
            in_specs=[pl.BlockSpec(memory_space=pl.ANY),
                      pl.BlockSpec((tps, n), row),
                      pl.BlockSpec((tps, LANES), row), vec, vec],
            out_specs=[pl.BlockSpec((tps, n), row), pl.BlockSpec((tps, n), row)],
            scratch_shapes=[pltpu.VMEM((TOP_K, tps, n), F32), pltpu.SemaphoreType.DMA(())]),
        out_shape=[jax.ShapeDtypeStruct((m, n), F32), jax.ShapeDtypeStruct((m, n), BF16)],
        compiler_params=_cparams(("arbitrary",)), name="moe_combine_ln",
    )(dest, ys, res, gates, g.reshape(1, n), b.reshape(1, n))


def _moe_block(x, xb, n_tokens, w_router, wg, wu, wd, g, b):
    del xb
    top_idx, gates = _route(x, w_router)
    n_sup = n_tokens * TOP_K // SUPER_ROWS + N_EXPERTS
    plan = _routing_plan(top_idx[:n_tokens, :TOP_K], n_sup)
    xs = _gather_rows(x, plan, n_sup)
    h = _moe_gate_up(xs, wg, wu, plan, n_sup)
    ys = _moe_down(h, wd, plan, n_sup)
    return _combine_residual_ln(ys, plan[0], gates, x, g, b, n_tokens)


def _stack_heads(q):
    return jnp.concatenate([q[:, h * HEAD_DIM:(h + 1) * HEAD_DIM] for h in range(KV_GROUP)], axis=0)


def _store_heads(o_ref, o):
    blk = o.shape[0] // KV_GROUP
    for h in range(KV_GROUP):
        o_ref[:, h * HEAD_DIM:(h + 1) * HEAD_DIM] = o[h * blk:(h + 1) * blk].astype(o_ref.dtype)


def _topk_lane_mask(gate, n_cand, n_valid):
    lane = lax.broadcasted_iota(jnp.int32, gate.shape, 1)
    cnt = jnp.zeros(gate.shape, jnp.int32)
    for c in range(n_cand):
        col = gate[:, c:c + 1]
        beats = (col > gate) | ((col == gate) & (c < lane))
        cnt = cnt + jnp.where(beats, jnp.where(c < n_valid, 1, 0), 0)
    return (lane < n_valid) & (cnt < MOBA_TOPK)


def _row_chunk(c):
    return pl.ds(pl.multiple_of(c * ROW_CHUNK, ROW_CHUNK), ROW_CHUNK)


def _chunk_positions(c, width):
    q_pos = (c * ROW_CHUNK + lax.broadcasted_iota(jnp.int32, (ROW_CHUNK, width), 0)) & (ATT_BLOCK - 1)
    return q_pos, lax.broadcasted_iota(jnp.int32, (ROW_CHUNK, width), 1)


def _lane_tile(x, width):
    return jnp.concatenate([x] * (width // LANES), axis=1) if width > LANES else x


def _moba_prompt_kernel(q_ref, k_ref, v_ref, o_ref, kmean_ref, qb_ref, gate_ref, bias_ref, s_ref, p_ref,
                        m_ref, l_ref, a_ref, acc_ref, *, n_blk):
    qi = pl.program_id(2)
    blk = ATT_BLOCK
    n_chunks = KV_GROUP * blk // ROW_CHUNK

    @pl.when(qi == 0)
    def _():
        kmean_ref[...] = jnp.zeros_like(kmean_ref)
        for j in range(n_blk):
            kmean_ref[j:j + 1, :] = jnp.mean(k_ref[j * blk:(j + 1) * blk, :], axis=0, keepdims=True)

    qr = _stack_heads(q_ref[...])
    qb_ref[...] = qr.astype(BF16)
    gate_ref[...] = _dot_nt(qr.astype(BF16), kmean_ref[...].astype(BF16))
    m_ref[...] = jnp.full(m_ref.shape, NEG_BIG, F32)
    l_ref[...] = jnp.zeros_like(l_ref)
    acc_ref[...] = jnp.zeros_like(acc_ref)

    def select(c, carry):
        r = _row_chunk(c)
        gate_ref[r, :] = _topk_lane_mask(gate_ref[r, :], n_blk - 1, qi).astype(F32)
        return carry

    lax.fori_loop(0, n_chunks, select, 0, unroll=CHUNK_UNROLL)

    n_past = n_blk - 1
    spread_row = lax.broadcasted_iota(jnp.int32, (LANES, n_past * LANES), 0)
    spread_col = lax.broadcasted_iota(jnp.int32, (LANES, n_past * LANES), 1)
    spread = jnp.where(spread_row == lax.shift_right_logical(spread_col, LANES.bit_length() - 1), 1.0, 0.0)
    replicated = _dot(gate_ref[...].astype(BF16), spread.astype(BF16))
    for j in range(n_past):
        bias_ref[j] = (replicated[:, j * LANES:(j + 1) * LANES] - 1.0) * (-NEG_BIG)
    ones = jnp.ones((blk, LANES), BF16)

    def attend(j, own):
        start = pl.multiple_of(j * blk, blk)
        s_ref[...] = _dot_nt(qb_ref[...], k_ref[pl.ds(start, blk), :].astype(BF16))

        def chunk(c, carry):
            r = _row_chunk(c)
            s = s_ref[r, :] * ATTN_SCALE
            if own:
                q_pos, k_pos = _chunk_positions(c, blk)
                s = jnp.where(k_pos <= q_pos, s, NEG_BIG)
            else:
                s = s + _lane_tile(bias_ref[j, r, :], blk)
            m_old = m_ref[r, :]
            m_new = jnp.maximum(m_old, jnp.max(s, axis=1, keepdims=True))
            m_ref[r, :] = m_new
            a_ref[r, :] = jnp.exp(m_old - m_new)
            p_ref[r, :] = jnp.exp(s - _lane_tile(m_new, blk)).astype(BF16)
            return carry

        lax.fori_loop(0, n_chunks, chunk, 0, unroll=CHUNK_UNROLL)
        v_ext = jnp.concatenate([v_ref[pl.ds(start, blk), :].astype(BF16), ones], axis=1)
        pv = _dot(p_ref[...], v_ext)
        a = a_ref[...]
        acc_ref[...] = a * acc_ref[...] + pv[:, :HEAD_DIM]
        l_ref[...] = a * l_ref[...] + pv[:, HEAD_DIM:]

    attend(qi, True)

    def past(j, carry):
        attend(j, False)
        return carry

    lax.fori_loop(0, qi, past, 0)
    _store_heads(o_ref, acc_ref[...] / l_ref[...])


def _moba_prompt(q, k, v, batch, seq):
    n_q = seq // ATT_BLOCK
    gw = KV_GROUP * HEAD_DIM
    rows = KV_GROUP * ATT_BLOCK
    kv_spec = pl.BlockSpec((seq, HEAD_DIM), lambda b, g, i: (b, g))
    return pl.pallas_call(
        functools.partial(_moba_prompt_kernel, n_blk=n_q),
        grid=(batch, N_KV_HEADS, n_q),
        in_specs=[pl.BlockSpec((ATT_BLOCK, gw), lambda b, g, i: (b * n_q + i, g)), kv_spec, kv_spec],
        out_specs=pl.BlockSpec((ATT_BLOCK, gw), lambda b, g, i: (b * n_q + i, g)),
        out_shape=jax.ShapeDtypeStruct((batch * seq, D_MODEL), BF16),
        scratch_shapes=[pltpu.VMEM((LANES, HEAD_DIM), F32),
                        pltpu.VMEM((rows, HEAD_DIM), BF16),
                        pltpu.VMEM((rows, LANES), F32),
                        pltpu.VMEM((n_q - 1, rows, LANES), F32),
                        pltpu.VMEM((rows, ATT_BLOCK), F32),
                        pltpu.VMEM((rows, ATT_BLOCK), BF16),
                        pltpu.VMEM((rows, LANES), F32), pltpu.VMEM((rows, LANES), F32),
                        pltpu.VMEM((rows, LANES), F32),
                        pltpu.VMEM((rows, HEAD_DIM), F32)],
        compiler_params=_cparams(("parallel", "parallel", "arbitrary")), name="moba_prompt")(q, k, v)


def _log_sigmoid_pair(z):
    t = jnp.log1p(jnp.exp(-jnp.abs(z)))
    return jnp.minimum(z, 0.0) - t, -jnp.maximum(z, 0.0) - t


def _suffix_sums(x, upper):
    hi = x.astype(BF16)
    lo = (x - hi.astype(F32)).astype(BF16)
    return _dot(hi, upper) + _dot(lo, upper)


def _strict_upper(n):
    r = lax.broadcasted_iota(jnp.int32, (n, n), 0)
    c = lax.broadcasted_iota(jnp.int32, (n, n), 1)
    return jnp.where(r > c, 1.0, 0.0).astype(BF16)


def _still_alive(carry_ref):
    return (jnp.max(carry_ref[...]) >= EXP_ZERO_BELOW).astype(jnp.int32)


def _sb_prompt_kernel(q_ref, k_ref, v_ref, o_ref, qb_ref, z_ref, lat_ref, hi_ref, lo_ref, w_ref,
                      carry_ref, rs_ref, acc_ref):
    qi = pl.program_id(2)
    blk = ATT_BLOCK
    n_chunks = KV_GROUP * blk // ROW_CHUNK
    qb_ref[...] = _stack_heads(q_ref[...]).astype(BF16)
    carry_ref[...] = jnp.zeros_like(carry_ref)
    acc_ref[...] = jnp.zeros_like(acc_ref)
    upper = _strict_upper(blk)

    def attend(j, diagonal):
        start = pl.multiple_of(j * blk, blk)
        z_ref[...] = _dot_nt(qb_ref[...], k_ref[pl.ds(start, blk), :].astype(BF16))

        def causal(c):
            q_pos, k_pos = _chunk_positions(c, blk)
            return k_pos < q_pos

        def keep_terms(c, carry):
            r = _row_chunk(c)
            log_beta, log_keep = _log_sigmoid_pair(z_ref[r, :] * ATTN_SCALE)
            if diagonal:
                log_keep = jnp.where(causal(c), log_keep, 0.0)
            hi = log_keep.astype(BF16)
            hi_ref[r, :] = hi
            lo_ref[r, :] = (log_keep - hi.astype(F32)).astype(BF16)
            z_ref[r, :] = log_beta + _lane_tile(carry_ref[r, :], blk)
            rs_ref[r, :] = jnp.broadcast_to(jnp.sum(log_keep, axis=1, keepdims=True), (ROW_CHUNK, LANES))
            return carry

        lax.fori_loop(0, n_chunks, keep_terms, 0, unroll=CHUNK_UNROLL)
        lat_ref[...] = _dot(hi_ref[...], upper) + _dot(lo_ref[...], upper)

        def weights(c, carry):
            r = _row_chunk(c)
            w = jnp.exp(z_ref[r, :] + lat_ref[r, :])
            if diagonal:
                w = jnp.where(causal(c), w, 0.0)
            w_ref[r, :] = w.astype(BF16)
            return carry

        lax.fori_loop(0, n_chunks, weights, 0, unroll=CHUNK_UNROLL)
        acc_ref[...] += _dot(w_ref[...], v_ref[pl.ds(start, blk), :].astype(BF16))
        carry_ref[...] += rs_ref[...]

    attend(qi, True)

    def more(state):
        t, alive = state
        return (t < qi) & (alive == 1)

    def older(state):
        t, _ = state
        attend(qi - 1 - t, False)
        return t + 1, _still_alive(carry_ref)

    lax.while_loop(more, older, (jnp.int32(0), _still_alive(carry_ref)))
    _store_heads(o_ref, acc_ref[...])


def _sb_prompt(q, k, v, batch, seq):
    n_q = seq // ATT_BLOCK
    gw = KV_GROUP * HEAD_DIM
    rows = KV_GROUP * ATT_BLOCK
    kv_spec = pl.BlockSpec((seq, HEAD_DIM), lambda b, g, i: (b, g))
    return pl.pallas_call(
        _sb_prompt_kernel, grid=(batch, N_KV_HEADS, n_q),
        in_specs=[pl.BlockSpec((ATT_BLOCK, gw), lambda b, g, i: (b * n_q + i, g)), kv_spec, kv_spec],
        out_specs=pl.BlockSpec((ATT_BLOCK, gw), lambda b, g, i: (b * n_q + i, g)),
        out_shape=jax.ShapeDtypeStruct((batch * seq, D_MODEL), BF16),
        scratch_shapes=[pltpu.VMEM((rows, HEAD_DIM), BF16),
                        pltpu.VMEM((rows, ATT_BLOCK), F32),
                        pltpu.VMEM((rows, ATT_BLOCK), F32),
                        pltpu.VMEM((rows, ATT_BLOCK), BF16), pltpu.VMEM((rows, ATT_BLOCK), BF16),
                        pltpu.VMEM((rows, ATT_BLOCK), BF16),
                        pltpu.VMEM((rows, LANES), F32), pltpu.VMEM((rows, LANES), F32),
                        pltpu.VMEM((rows, HEAD_DIM), F32)],
        compiler_params=_cparams(("parallel", "parallel", "arbitrary")), name="sb_prompt")(q, k, v)


def _group_rows(q, dec_batch, dec_seq):
    q = q.reshape(dec_batch, dec_seq, N_KV_HEADS, KV_GROUP, HEAD_DIM)
    return q.transpose(0, 2, 1, 3, 4).reshape(dec_batch, N_KV_HEADS, dec_seq * KV_GROUP, HEAD_DIM)


def _ungroup_rows(o, dec_batch, dec_seq):
    o = o.reshape(dec_batch, N_KV_HEADS, dec_seq, KV_GROUP, HEAD_DIM)
    return o.transpose(0, 2, 1, 3, 4).reshape(dec_batch * dec_seq, D_MODEL)


def _pad_new(x, dec_batch, dec_seq):
    x = x.reshape(dec_batch, dec_seq, KV_WIDTH)
    return jnp.pad(x, ((0, 0), (0, PAGE_SIZE - dec_seq), (0, 0)))


def _page_pair(ref0, ref1, g):
    return jnp.concatenate([ref0[:, g, :], ref1[:, g, :]], axis=0).astype(BF16)


def _moba_sample_kernel(pt_ref, q_ref, kn_ref, vn_ref, *refs, n_blk, dec_seq):
    del pt_ref
    pages = PAGES_PER_STEP
    k_refs, v_refs = refs[:pages], refs[pages:2 * pages]
    o_ref, s_ref, p_ref, kmean_ref, acc_ref, l_ref = refs[2 * pages:]
    step = pl.program_id(1)
    rg = dec_seq * KV_GROUP
    bps = pages // 2
    k_steps = n_blk // bps

    @pl.when(step == 0)
    def _():
        kmean_ref[...] = jnp.zeros_like(kmean_ref)

    @pl.when(step < k_steps)
    def _():
        for g in range(N_KV_HEADS):
            qb = q_ref[g].astype(BF16)
            for bl in range(bps):
                j = step * bps + bl
                kf = jnp.concatenate([k_refs[2 * bl][:, g, :], k_refs[2 * bl + 1][:, g, :]], axis=0)
                kmean_ref[g, pl.ds(j, 1), :] = jnp.mean(kf, axis=0, keepdims=True)
                s_ref[j, g * rg:(g + 1) * rg, :] = _dot_nt(qb, kf.astype(BF16)) * ATTN_SCALE

    @pl.when(step == k_steps)
    def _():
        sel_g, s_new_g = [], []
        tok = lax.broadcasted_iota(jnp.int32, (rg, PAGE_SIZE), 0) // KV_GROUP
        key = lax.broadcasted_iota(jnp.int32, (rg, PAGE_SIZE), 1)
        for g in range(N_KV_HEADS):
            gate = _dot_nt(q_ref[g].astype(BF16), kmean_ref[g].astype(BF16))
            sel_g.append(_topk_lane_mask(gate, n_blk, n_blk).astype(F32))
            kn = kn_ref[:, g * HEAD_DIM:(g + 1) * HEAD_DIM].astype(BF16)
            s_new = _dot_nt(q_ref[g].astype(BF16), kn) * ATTN_SCALE
            s_new_g.append(jnp.where(key <= tok, s_new, NEG_BIG))
        sel = jnp.concatenate(sel_g, axis=0)
        s_new = jnp.concatenate(s_new_g, axis=0)
        m = jnp.max(s_new, axis=1, keepdims=True)
        for j in range(n_blk):
            picked = sel[:, j:j + 1] > 0.5
            m = jnp.maximum(m, jnp.max(jnp.where(picked, s_ref[j], NEG_BIG), axis=1, keepdims=True))
        p_new = jnp.exp(s_new - m)
        l = jnp.sum(p_new, axis=1, keepdims=True)
        for j in range(n_blk):
            picked = sel[:, j:j + 1] > 0.5
            p = jnp.where(picked, jnp.exp(s_ref[j] - m), 0.0)
            l = l + jnp.sum(p, axis=1, keepdims=True)
            p_ref[j] = p.astype(BF16)
        l_ref[...] = l
        for g in range(N_KV_HEADS):
            vn = vn_ref[:, g * HEAD_DIM:(g + 1) * HEAD_DIM].astype(BF16)
            acc_ref[g * rg:(g + 1) * rg, :] = _dot(p_new[g * rg:(g + 1) * rg].astype(BF16), vn)

    @pl.when(step >= k_steps)
    def _():
        for g in range(N_KV_HEADS):
            part = None
            for bl in range(bps):
                j = (step - k_steps) * bps + bl
                t = _dot(p_ref[j, g * rg:(g + 1) * rg, :], _page_pair(v_refs[2 * bl], v_refs[2 * bl + 1], g))
                part = t if part is None else part + t
            acc_ref[g * rg:(g + 1) * rg, :] += part

    @pl.when(step == 2 * k_steps - 1)
    def _():
        out = acc_ref[...] / l_ref[...]
        for g in range(N_KV_HEADS):
            o_ref[g] = out[g * rg:(g + 1) * rg]


def _moba_sample(qg, k_new, v_new, cache_k, cache_v, page_table, dec_seq):
    dec_batch, n_pages = page_table.shape
    n_blk = n_pages // 2
    rg = dec_seq * KV_GROUP
    rows = N_KV_HEADS * rg
    pages = PAGES_PER_STEP
    assert n_pages % pages == 0
    k_steps = n_pages // pages
    page = (None, PAGE_SIZE, N_KV_HEADS, HEAD_DIM)

    def k_map(off):
        return lambda b, s, pt: (pt[b, pages * jnp.minimum(s, k_steps - 1) + off], 0, 0, 0)

    def v_map(off):
        return lambda b, s, pt: (pt[b, pages * jnp.maximum(s - k_steps, 0) + off], 0, 0, 0)

    per_b4 = pl.BlockSpec((None, N_KV_HEADS, rg, HEAD_DIM), lambda b, s, pt: (b, 0, 0, 0))
    per_b3 = pl.BlockSpec((None, PAGE_SIZE, KV_WIDTH), lambda b, s, pt: (b, 0, 0))
    return pl.pallas_call(
        functools.partial(_moba_sample_kernel, n_blk=n_blk, dec_seq=dec_seq),
        grid_spec=pltpu.PrefetchScalarGridSpec(
            num_scalar_prefetch=1, grid=(dec_batch, 2 * k_steps),
            in_specs=[per_b4, per_b3, per_b3]
            + [pl.BlockSpec(page, k_map(i)) for i in range(pages)]
            + [pl.BlockSpec(page, v_map(i)) for i in range(pages)],
            out_specs=per_b4,
            scratch_shapes=[pltpu.VMEM((n_blk, rows, 2 * PAGE_SIZE), F32),
                            pltpu.VMEM((n_blk, rows, 2 * PAGE_SIZE), BF16),
                            pltpu.VMEM((N_KV_HEADS, LANES, HEAD_DIM), F32),
                            pltpu.VMEM((rows, HEAD_DIM), F32),
                            pltpu.VMEM((rows, 1), F32)]),
        out_shape=jax.ShapeDtypeStruct((dec_batch, N_KV_HEADS, rg, HEAD_DIM), F32),
        compiler_params=_cparams(("parallel", "arbitrary")), name="moba_sample",
    )(page_table, qg, k_new, v_new, *([cache_k] * pages), *([cache_v] * pages))


def _sb_sample_kernel(pt_ref, q_ref, kn_ref, vn_ref, *refs, dec_seq):
    del pt_ref
    pages = PAGES_PER_STEP
    k_refs, v_refs = refs[:pages], refs[pages:2 * pages]
    o_ref, acc_ref, carry_ref, alive_ref = refs[2 * pages:]
    step = pl.program_id(1)
    rg = dec_seq * KV_GROUP
    blk = 2 * PAGE_SIZE

    def accumulate(g, kb, vb, upper, causal):
        rows = slice(g * rg, (g + 1) * rg)
        z = _dot_nt(q_ref[g].astype(BF16), kb) * ATTN_SCALE
        log_beta, log_keep = _log_sigmoid_pair(z)
        if causal is not None:
            log_keep = jnp.where(causal, log_keep, 0.0)
        w = jnp.exp(log_beta + _suffix_sums(log_keep, upper) + carry_ref[rows, :])
        if causal is not None:
            w = jnp.where(causal, w, 0.0)
        acc_ref[rows, :] += _dot(w.astype(BF16), vb)
        carry_ref[rows, :] += jnp.sum(log_keep, axis=1, keepdims=True)

    @pl.when(step == 0)
    def _():
        acc_ref[...] = jnp.zeros_like(acc_ref)
        carry_ref[...] = jnp.zeros_like(carry_ref)
        tok = lax.broadcasted_iota(jnp.int32, (rg, PAGE_SIZE), 0) // KV_GROUP
        key = lax.broadcasted_iota(jnp.int32, (rg, PAGE_SIZE), 1)
        upper = _strict_upper(PAGE_SIZE)
        for g in range(N_KV_HEADS):
            cols = slice(g * HEAD_DIM, (g + 1) * HEAD_DIM)
            accumulate(g, kn_ref[:, cols].astype(BF16), vn_ref[:, cols].astype(BF16), upper, key < tok)
        alive_ref[0] = _still_alive(carry_ref)

    @pl.when(alive_ref[0] == 1)
    def _():
        upper = _strict_upper(blk)
        for bl in reversed(range(pages // 2)):
            for g in range(N_KV_HEADS):
                accumulate(g, _page_pair(k_refs[2 * bl], k_refs[2 * bl + 1], g),
                           _page_pair(v_refs[2 * bl], v_refs[2 * bl + 1], g), upper, None)
        alive_ref[0] = _still_alive(carry_ref)

    @pl.when(step == pl.num_programs(1) - 1)
    def _():
        for g in range(N_KV_HEADS):
            o_ref[g] = acc_ref[g * rg:(g + 1) * rg, :]


def _sb_sample(qg, k_new, v_new, cache_k, cache_v, page_table, dec_seq):
    dec_batch, n_pages = page_table.shape
    rg = dec_seq * KV_GROUP
    rows = N_KV_HEADS * rg
    pages = PAGES_PER_STEP
    assert n_pages % pages == 0
    page = (None, PAGE_SIZE, N_KV_HEADS, HEAD_DIM)

    def page_map(off):
        return lambda b, s, pt: (pt[b, n_pages - pages * (s + 1) + off], 0, 0, 0)

    per_b4 = pl.BlockSpec((None, N_KV_HEADS, rg, HEAD_DIM), lambda b, s, pt: (b, 0, 0, 0))
    per_b3 = pl.BlockSpec((None, PAGE_SIZE, KV_WIDTH), lambda b, s, pt: (b, 0, 0))
    page_specs = [pl.BlockSpec(page, page_map(i)) for i in range(pages)]
    return pl.pallas_call(
        functools.partial(_sb_sample_kernel, dec_seq=dec_seq),
        grid_spec=pltpu.PrefetchScalarGridSpec(
            num_scalar_prefetch=1, grid=(dec_batch, n_pages // pages),
            in_specs=[per_b4, per_b3, per_b3] + page_specs + page_specs,
            out_specs=per_b4,
            scratch_shapes=[pltpu.VMEM((rows, HEAD_DIM), F32), pltpu.VMEM((rows, 1), F32),
                            pltpu.SMEM((1,), jnp.int32)]),
        out_shape=jax.ShapeDtypeStruct((dec_batch, N_KV_HEADS, rg, HEAD_DIM), F32),
        compiler_params=_cparams(("parallel", "arbitrary")), name="sb_sample",
    )(page_table, qg, k_new, v_new, *([cache_k] * pages), *([cache_v] * pages))


def _pool_kernel(x_ref, w_ref, scale_ref, o_ref, *, seq, chunk, counts):
    g = pl.program_id(1)
    halo = POOL_STATE + 1
    w = w_ref[...].astype(BF16)
    for c0 in range(0, seq, chunk):
        cur = x_ref[halo + c0:halo + c0 + chunk, :]
        tot, out = cur, None
        for d in range(1, max(POOL_WINDOWS)):
            tot = tot + x_ref[halo + c0 - d:halo + c0 - d + chunk, :]
            if d + 1 in POOL_WINDOWS:
                gi = POOL_WINDOWS.index(d + 1)
                mixed = tot / counts(gi, c0, chunk) - cur
                out = mixed if out is None else jnp.where(g == gi, mixed, out)
        o_ref[c0:c0 + chunk, :] = _dot(out.astype(BF16), w) * scale_ref[...]


def _pool_mixer(x_hist, w_pool, scale, seq, first_pos):
    batch = x_hist.shape[0]
    halo = POOL_STATE + 1
    chunk = min(seq, 512)

    def counts(gi, c0, rows):
        win = POOL_WINDOWS[gi]
        pos = first_pos + c0 + lax.broadcasted_iota(jnp.int32, (rows, 1), 0)
        return jnp.minimum(win, pos + 1).astype(F32)

    return pl.pallas_call(
        functools.partial(_pool_kernel, seq=seq, chunk=chunk, counts=counts),
        grid=(batch, len(POOL_WINDOWS)),
        in_specs=[pl.BlockSpec((None, halo + seq, POOL_GROUP), lambda b, g: (b, 0, g)),
                  pl.BlockSpec((None, POOL_GROUP, POOL_GROUP), lambda b, g: (g, 0, 0)),
                  pl.BlockSpec((1, POOL_GROUP), lambda b, g: (0, g))],
        out_specs=pl.BlockSpec((None, seq, POOL_GROUP), lambda b, g: (b, 0, g)),
        out_shape=jax.ShapeDtypeStruct((batch, seq, D_MODEL), F32),
        compiler_params=_cparams(("parallel", "parallel")), name="pool_mixer",
    )(x_hist, w_pool, scale.reshape(1, D_MODEL))


def _rope_tables(pos):
    half = ROT_DIM // 2
    inv = ROPE_THETA ** (-jnp.arange(0, ROT_DIM, 2, dtype=F32) / ROT_DIM)
    ang = pos.astype(F32)[:, None] * inv[None, :]
    cos, sin = jnp.cos(ang), jnp.sin(ang)
    n = pos.shape[0]
    ones = jnp.ones((n, HEAD_DIM - ROT_DIM), F32)
    zeros = jnp.zeros((n, HEAD_DIM - half), F32)
    a = jnp.concatenate([cos, cos, ones], axis=1)
    b = jnp.concatenate([-sin, zeros], axis=1)
    c = jnp.concatenate([jnp.zeros((n, half), F32), sin, jnp.zeros((n, HEAD_DIM - ROT_DIM), F32)], axis=1)
    return a, b, c


def kernel(x_prompt, x_sample, cache_k_l0, cache_v_l0, state_pool_l1, cache_k_l2, cache_v_l2, cache_k_l3, cache_v_l3, page_table, l0_wq, l0_wk, l0_wv, l0_wo, l0_ln1_g, l0_ln1_b, l0_ffn_wg, l0_ffn_wu, l0_ffn_wd, l0_ln2_g, l0_ln2_b, l1_pool_w, l1_pool_scale, l1_ln1_g, l1_ln1_b, l1_router, l1_moe_wg, l1_moe_wu, l1_moe_wd, l1_ln2_g, l1_ln2_b, l2_wq, l2_wk, l2_wv, l2_wo, l2_ln1_g, l2_ln1_b, l2_ffn_wg, l2_ffn_wu, l2_ffn_wd, l2_ln2_g, l2_ln2_b, l3_wq, l3_wk, l3_wv, l3_wo, l3_ln1_g, l3_ln1_b, l3_router, l3_moe_wg, l3_moe_wu, l3_moe_wd, l3_ln2_g, l3_ln2_b):
    batch, seq, _ = x_prompt.shape
    dec_batch, dec_seq, _ = x_sample.shape
    n_prompt = batch * seq
    n_sample = dec_batch * dec_seq
    n_tokens = n_prompt + n_sample
    m_pad = -(-n_tokens // ROW_TILE) * ROW_TILE
    assert m_pad % LN_ROW_TILE == 0 and seq % ATT_BLOCK == 0 and PAST_LEN % MOBA_BLOCK == 0
    assert PAST_LEN // MOBA_BLOCK == (PAST_LEN + dec_seq - 1) // MOBA_BLOCK and dec_seq <= PAGE_SIZE

    x = jnp.concatenate([x_prompt.reshape(n_prompt, D_MODEL), x_sample.reshape(n_sample, D_MODEL),
                         jnp.zeros((m_pad - n_tokens, D_MODEL), F32)], axis=0)
    xb = x.astype(BF16)
    pos = jnp.concatenate([jnp.tile(jnp.arange(seq, dtype=jnp.int32), batch),
                           jnp.tile(PAST_LEN + jnp.arange(dec_seq, dtype=jnp.int32), dec_batch),
                           jnp.zeros((m_pad - n_tokens,), jnp.int32)])
    rope = _rope_tables(pos)

    def split(y, width):
        p = y[:n_prompt].reshape(batch, seq, width)
        s = y[n_prompt:n_tokens].reshape(dec_batch, dec_seq, width)
        return p, s

    def attention_layer(kind, x, xb, w, caches):
        wq, wk, wv, wo = w
        q = _project(xb, wq, rope if kind == 0 else None)
        k = _project(xb, wk, rope if kind == 0 else None)
        v = _project(xb, wv)
        prompt_fn = _moba_prompt if kind == 0 else _sb_prompt
        sample_fn = _moba_sample if kind == 0 else _sb_sample
        o_p = prompt_fn(q, k, v, batch, seq)
        qg = _group_rows(q[n_prompt:n_tokens], dec_batch, dec_seq)
        o_s = sample_fn(qg, _pad_new(k[n_prompt:n_tokens], dec_batch, dec_seq),
                        _pad_new(v[n_prompt:n_tokens], dec_batch, dec_seq), caches[0], caches[1], page_table, dec_seq)
        o_s = _ungroup_rows(o_s, dec_batch, dec_seq).astype(BF16)
        o = jnp.concatenate([o_p, o_s, jnp.zeros((m_pad - n_tokens, D_MODEL), BF16)], axis=0)
        kp, ks = split(k, KV_WIDTH)
        vp, vs = split(v, KV_WIDTH)
        shape4 = lambda t: t.reshape(t.shape[0], t.shape[1], N_KV_HEADS, HEAD_DIM)
        return o, wo, (shape4(kp), shape4(vp), shape4(ks), shape4(vs))

    def pool_layer(x):
        xp, xs = split(x, D_MODEL)
        hist_p = jnp.concatenate([jnp.zeros((batch, POOL_STATE + 1, D_MODEL), F32), xp], axis=1)
        seq_s = -(-dec_seq // SUBLANES) * SUBLANES
        hist_s = jnp.concatenate([jnp.zeros((dec_batch, 1, D_MODEL), F32), state_pool_l1, xs,
                                  jnp.zeros((dec_batch, seq_s - dec_seq, D_MODEL), F32)], axis=1)
        h_p = _pool_mixer(hist_p, l1_pool_w, l1_pool_scale, seq, 0)
        h_s = _pool_mixer(hist_s, l1_pool_w, l1_pool_scale, seq_s, PAST_LEN)[:, :dec_seq]
        h = jnp.concatenate([h_p.reshape(n_prompt, D_MODEL), h_s.reshape(n_sample, D_MODEL),
                             jnp.zeros((m_pad - n_tokens, D_MODEL), F32)], axis=0)
        state_p = hist_p[:, -POOL_STATE:]
        state_s = jnp.concatenate([state_pool_l1, xs], axis=1)[:, -POOL_STATE:]
        return h, (state_p, state_s)

    def dense_ffn(x, xb, w, g, b):
        wg, wu, wd = w
        return _down_residual_ln(_gate_up(xb, wg, wu), wd, x, g, b)

    o, wo, kv0 = attention_layer(0, x, xb, (l0_wq, l0_wk, l0_wv, l0_wo), (cache_k_l0, cache_v_l0))
    x, xb = _down_residual_ln(o, wo, x, l0_ln1_g, l0_ln1_b)
    x, xb = dense_ffn(x, xb, (l0_ffn_wg, l0_ffn_wu, l0_ffn_wd), l0_ln2_g, l0_ln2_b)
    h, pool_state = pool_layer(x)
    x, xb = _residual_ln(x, h, l1_ln1_g, l1_ln1_b)
    x, xb = _moe_block(x, xb, n_tokens, l1_router, l1_moe_wg, l1_moe_wu, l1_moe_wd, l1_ln2_g, l1_ln2_b)
    o, wo, kv2 = attention_layer(2, x, xb, (l2_wq, l2_wk, l2_wv, l2_wo), (cache_k_l2, cache_v_l2))
    x, xb = _down_residual_ln(o, wo, x, l2_ln1_g, l2_ln1_b)
    x, xb = dense_ffn(x, xb, (l2_ffn_wg, l2_ffn_wu, l2_ffn_wd), l2_ln2_g, l2_ln2_b)
    o, wo, kv3 = attention_layer(0, x, xb, (l3_wq, l3_wk, l3_wv, l3_wo), (cache_k_l3, cache_v_l3))
    x, xb = _down_residual_ln(o, wo, x, l3_ln1_g, l3_ln1_b)
    x, xb = _moe_block(x, xb, n_tokens, l3_router, l3_moe_wg, l3_moe_wu, l3_moe_wd, l3_ln2_g, l3_ln2_b)

    y_prompt, y_sample = split(x, D_MODEL)
    return (y_prompt, y_sample, kv0[0], kv0[1], kv0[2], kv0[3], pool_state[0], pool_state[1],
            kv2[0], kv2[1], kv2[2], kv2[3], kv3[0], kv3[1], kv3[2], kv3[3])
```
